```python
import jax, jax.numpy as jnp
from jax import lax
import numpy as np

D_MODEL = 1024
BATCH = 8
SEQ = 2048
DEPTH = 1
DEC_BATCH = 4
DEC_SEQ = 8192
PAST_LEN = 128

N_META = 16
GRID_W = 64
EXPAND = 2
D_MIX = EXPAND * D_MODEL
D_POOL = D_MIX // 2
D_NA = D_MIX - D_POOL
POOL_WINDOWS = (2, 4, 8, 16)
N_POOL_GROUPS = len(POOL_WINDOWS)
POOL_GROUP_W = D_POOL // N_POOL_GROUPS
NA_HEAD_DIM = 64
NA_HEADS = D_NA // NA_HEAD_DIM
NA_KR_MAX = 8
NA_KC = 16
RMS_EPS = 1e-6
D_IN = 2 * D_POOL + 4 * D_NA

kernel_name = "hymba_pool_natten_encoder"


def rms_norm(x, g):
    xf = x.astype(jnp.float32)
    y = xf * lax.rsqrt(jnp.mean(xf * xf, axis=-1, keepdims=True) + RMS_EPS)
    return (y * g.astype(jnp.float32)).astype(x.dtype)


def pool_mixer(u, w_pool, pool_scale):
    B, L, _ = u.shape
    ug = u.reshape(B, L, N_POOL_GROUPS, POOL_GROUP_W)
    ugf = ug.astype(jnp.float32)
    cs = jnp.concatenate([jnp.zeros((B, 1, N_POOL_GROUPS, POOL_GROUP_W), jnp.float32),
                          jnp.cumsum(ugf, axis=1)], axis=1)
    w = jnp.array(POOL_WINDOWS, jnp.int32)
    t = jnp.arange(L, dtype=jnp.int32)
    start = t[:, None] - w[None, :] // 2
    lo = jnp.clip(start, 0, L)
    hi = jnp.clip(start + w[None, :], 0, L)
    gidx = jnp.arange(N_POOL_GROUPS)[None, :]
    win_sum = cs[:, hi, gidx, :] - cs[:, lo, gidx, :]
    count = (hi - lo).astype(jnp.float32)[None, :, :, None]
    pooled = (win_sum / count - ugf).astype(u.dtype)
    mixed = jnp.einsum('blgc,gcd->blgd', pooled, w_pool).reshape(B, L, D_POOL)
    return mixed * pool_scale


def neighborhood_attention(q, k, v, rpb, meta_bias):
    B, L, H, hd = q.shape
    T = L - N_META
    rows = T // GRID_W
    kr_n = min(NA_KR_MAX, rows)
    scale = hd ** -0.5
    qm, km, vm = q[:, :N_META], k[:, :N_META], v[:, :N_META]
    qg = q[:, N_META:].reshape(B, rows, GRID_W, H, hd)
    kg = k[:, N_META:].reshape(B, rows, GRID_W, H, hd)
    vg = v[:, N_META:].reshape(B, rows, GRID_W, H, hd)
    cols = np.arange(GRID_W)
    cstart = np.clip(cols - NA_KC // 2, 0, GRID_W - NA_KC)
    colidx = cstart[:, None] + np.arange(NA_KC)[None, :]
    coloff = colidx - cols[:, None] + (NA_KC - 1)
    mbias = meta_bias.astype(jnp.float32)[None, :, None, :]

    def row_block(r):
        rs = jnp.clip(r - kr_n // 2, 0, rows - kr_n)
        q_r = lax.dynamic_index_in_dim(qg, r, axis=1, keepdims=False)
        k_r = lax.dynamic_slice_in_dim(kg, rs, kr_n, axis=1)[:, :, colidx]
        v_r = lax.dynamic_slice_in_dim(vg, rs, kr_n, axis=1)[:, :, colidx]
        rowoff = rs + jnp.arange(kr_n) - r + (NA_KR_MAX - 1)
        bias = rpb[:, rowoff][:, :, coloff]
        bias = bias.transpose(0, 2, 1, 3).astype(jnp.float32)[None]
        s_grid = jnp.einsum('bqhd,bkqjhd->bhqkj', q_r, k_r).astype(jnp.float32) * scale + bias
        s_grid = s_grid.reshape(B, H, GRID_W, kr_n * NA_KC)
        s_meta = jnp.einsum('bqhd,bmhd->bhqm', q_r, km).astype(jnp.float32) * scale + mbias
        p = jax.nn.softmax(jnp.concatenate([s_meta, s_grid], axis=-1), axis=-1).astype(v.dtype)
        p_m = p[..., :N_META]
        p_g = p[..., N_META:].reshape(B, H, GRID_W, kr_n, NA_KC)
        return (jnp.einsum('bhqm,bmhd->bqhd', p_m, vm)
                + jnp.einsum('bhqkj,bkqjhd->bqhd', p_g, v_r))

    out_g = lax.map(row_block, jnp.arange(rows))
    out_g = out_g.transpose(1, 0, 2, 3, 4).reshape(B, T, H, hd)
    s_mm = jnp.einsum('bqhd,bmhd->bhqm', qm, km).astype(jnp.float32) * scale + mbias
    p_mm = jax.nn.softmax(s_mm, axis=-1).astype(v.dtype)
    out_m = jnp.einsum('bhqm,bmhd->bqhd', p_mm, vm)
    return jnp.concatenate([out_m, out_g], axis=1).reshape(B, L, H * hd)


def mixer_layer(x, norm_g, w_in, w_pool, pool_scale, rpb, meta_bias, w_out):
    B, L, _ = x.shape
    h = rms_norm(x, norm_g)
    proj = h @ w_in
    splits = [D_POOL, 2 * D_POOL, 2 * D_POOL + D_NA, 2 * D_POOL + 2 * D_NA, 2 * D_POOL + 3 * D_NA]
    u, g_pool, q, k, v, g_na = jnp.split(proj, splits, axis=-1)
    pool_out = pool_mixer(u, w_pool, pool_scale) * jax.nn.silu(g_pool)
    shp = (B, L, NA_HEADS, NA_HEAD_DIM)
    na_out = neighborhood_attention(q.reshape(shp), k.reshape(shp), v.reshape(shp),
                                    rpb, meta_bias) * jax.nn.silu(g_na)
    return x + jnp.concatenate([pool_out, na_out], axis=-1) @ w_out


def encode(x, meta_tokens, norm_g, w_in, w_pool, pool_scale, rpb, meta_bias, w_out, final_g):
    B = x.shape[0]
    meta = jnp.broadcast_to(meta_tokens[None].astype(x.dtype), (B, N_META, x.shape[-1]))
    h = jnp.concatenate([meta, x], axis=1)
    for l in range(DEPTH):
        h = mixer_layer(h, norm_g[l], w_in[l], w_pool[l], pool_scale[l], rpb[l],
                        meta_bias[l], w_out[l])
    return rms_norm(h, final_g)[:, N_META:]


def setup_inputs(seed: int = 0) -> dict:
    key = jax.random.key(seed)
    ks = jax.random.split(key, 12)
    f32 = jnp.float32
    return {
        "x_prompt": jax.random.normal(ks[0], (BATCH, SEQ, D_MODEL), f32),
        "x_sample": jax.random.normal(ks[1], (DEC_BATCH, DEC_SEQ, D_MODEL), f32),
        "meta_tokens": jax.random.normal(ks[2], (N_META, D_MODEL), f32),
        "norm_g": 1.0 + 0.02 * jax.random.normal(ks[3], (DEPTH, D_MODEL), f32),
        "w_in": jax.random.normal(ks[4], (DEPTH, D_MODEL, D_IN), f32) * D_MODEL ** -0.5,
        "w_pool": jax.random.normal(ks[5], (DEPTH, N_POOL_GROUPS, POOL_GROUP_W, POOL_GROUP_W), f32) * POOL_GROUP_W ** -0.5,
        "pool_scale": 1.0 + 0.02 * jax.random.normal(ks[6], (DEPTH, D_POOL), f32),
        "rpb": 0.02 * jax.random.normal(ks[7], (DEPTH, NA_HEADS, 2 * NA_KR_MAX - 1, 2 * NA_KC - 1), f32),
        "meta_bias": 0.02 * jax.random.normal(ks[8], (DEPTH, NA_HEADS, N_META), f32),
        "w_out": jax.random.normal(ks[9], (DEPTH, D_MIX, D_MODEL), f32) * D_MIX ** -0.5,
        "final_g": 1.0 + 0.02 * jax.random.normal(ks[10], (D_MODEL,), f32),
    }


def reference(x_prompt, x_sample, meta_tokens, norm_g, w_in, w_pool, pool_scale, rpb,
              meta_bias, w_out, final_g):
    y_prompt = encode(x_prompt, meta_tokens, norm_g, w_in, w_pool, pool_scale, rpb,
                      meta_bias, w_out, final_g)
    y_sample = encode(x_sample, meta_tokens, norm_g, w_in, w_pool, pool_scale, rpb,
                      meta_bias, w_out, final_g)
    return (y_prompt, y_sample)
```

```python
import functools

import jax
import jax.numpy as jnp
from jax import lax
from jax.experimental import pallas as pl
from jax.experimental.pallas import tpu as pltpu

D_MODEL = 1024
N_META = 16
GRID_W = 64
D_POOL = 1024
D_NA = 1024
D_MIX = D_POOL + D_NA
D_IN = 2 * D_POOL + 4 * D_NA
POOL_WINDOWS = (2, 4, 8, 16)
POOL_GROUP_W = D_POOL // len(POOL_WINDOWS)
HEAD_DIM = 64
N_HEADS = D_NA // HEAD_DIM
NA_ROWS = 8
NA_COLS = 16
RMS_EPS = 1e-6
MASKED = -1e30

LANES = 128
MXU_WIDTH = 256
VMEM_LIMIT_BYTES = 60 * 1024 * 1024

HEADS_PER_STACK = MXU_WIDTH // HEAD_DIM
STACK_W = HEADS_PER_STACK * HEAD_DIM
N_STACKS = N_HEADS // HEADS_PER_STACK
META_PAD = LANES
HALO_ROWS = NA_ROWS // 2
POOL_HALO = max(POOL_WINDOWS)
PAIR_OFFSETS = 2 * NA_ROWS - 2


def _sigmoid(x):
    return 1.0 / (1.0 + jnp.exp(-x))


def _in_proj_kernel(x_ref, g_ref, w_ref, u_ref, gp_ref, q_ref, k_ref, v_ref, gn_ref):
    x = x_ref[...]
    ms = jnp.mean(x * x, axis=-1, keepdims=True)
    h = (x * lax.rsqrt(ms + RMS_EPS) * g_ref[...]).astype(jnp.bfloat16)
    outs = (u_ref, gp_ref, q_ref, k_ref, v_ref, gn_ref)
    for n, o_ref in enumerate(outs):
        width = o_ref.shape[-1]
        acc = jnp.dot(h, w_ref[:, n * width:(n + 1) * width], preferred_element_type=jnp.float32)
        if o_ref is q_ref:
            acc = acc * (HEAD_DIM ** -0.5)
        o_ref[...] = acc.astype(o_ref.dtype)


def _in_proj(x2d, norm_g, w_in_bf16, block_tokens):
    tokens = x2d.shape[0]
    assert tokens % block_tokens == 0
    out = jax.ShapeDtypeStruct((tokens, D_POOL), jnp.bfloat16)
    row_spec = pl.BlockSpec((block_tokens, D_MODEL), lambda i: (i, 0))
    return pl.pallas_call(
        _in_proj_kernel,
        grid=(tokens // block_tokens,),
        in_specs=[
            row_spec,
            pl.BlockSpec((1, D_MODEL), lambda i: (0, 0)),
            pl.BlockSpec((D_MODEL, D_IN), lambda i: (0, 0), pipeline_mode=pl.Buffered(1)),
        ],
        out_specs=[row_spec] * 6,
        out_shape=[out] * 6,
        compiler_params=pltpu.CompilerParams(
            dimension_semantics=("arbitrary",), vmem_limit_bytes=VMEM_LIMIT_BYTES),
        name="in_proj",
    )(x2d, norm_g, w_in_bf16)


def _mixer_kernel(q_ref, kp_ref, k_ref, kn_ref, vp_ref, v_ref, vn_ref, up_ref, u_ref, un_ref,
                  gp_ref, gn_ref, x_ref, um_ref, km_ref, vm_ref, bias_ref, biasm_ref,
                  wpool_ref, pscale_ref, wout_ref, fg_ref, o_ref,
                  kwin, vwin, ubuf, pooled, mix, *, rows, block_rows):
    f32, bf16 = jnp.float32, jnp.bfloat16
    blk_tokens = block_rows * GRID_W
    halo_tokens = HALO_ROWS * GRID_W
    seq_tokens = rows * GRID_W
    i = pl.program_id(1)
    is_first = i == 0
    is_last = i == pl.num_programs(1) - 1

    for win, prev, main, nxt in ((kwin, kp_ref, k_ref, kn_ref), (vwin, vp_ref, v_ref, vn_ref)):
        win[0:halo_tokens] = prev[...]
        win[halo_tokens:halo_tokens + blk_tokens] = main[...]
        win[halo_tokens + blk_tokens:] = nxt[...]

    prev_tail = up_ref[halo_tokens - POOL_HALO:halo_tokens, :].astype(f32)
    ubuf[0:POOL_HALO] = jnp.where(is_first, um_ref[...].astype(f32), prev_tail)
    ubuf[POOL_HALO:POOL_HALO + blk_tokens] = u_ref[...].astype(f32)
    next_head = un_ref[0:POOL_HALO, :].astype(f32)
    ubuf[POOL_HALO + blk_tokens:] = jnp.where(is_last, 0.0, next_head)

    for c in range(block_rows):
        t0 = c * GRID_W
        tok = i * blk_tokens + t0 + lax.broadcasted_iota(jnp.int32, (GRID_W, 1), 0)
        for g, w in enumerate(POOL_WINDOWS):
            lanes = slice(g * POOL_GROUP_W, (g + 1) * POOL_GROUP_W)
            base = POOL_HALO + t0
            acc = ubuf[base - w // 2:base - w // 2 + GRID_W, lanes]
            for s in range(-w // 2 + 1, w // 2):
                acc = acc + ubuf[base + s:base + s + GRID_W, lanes]
            count = jnp.minimum(w, seq_tokens - tok + w // 2).astype(f32)
            centre = ubuf[base:base + GRID_W, lanes]
            pooled[t0:t0 + GRID_W, lanes] = (acc / count - centre).astype(bf16)

    for g in range(len(POOL_WINDOWS)):
        lanes = slice(g * POOL_GROUP_W, (g + 1) * POOL_GROUP_W)
        mixed = jnp.dot(pooled[:, lanes], wpool_ref[g], preferred_element_type=f32)
        gate = gp_ref[:, lanes].astype(f32)
        mix[:, lanes] = (mixed * pscale_ref[:, lanes] * (gate * _sigmoid(gate))).astype(bf16)

    lane_head = lax.broadcasted_iota(jnp.int32, (1, STACK_W), 1) // HEAD_DIM
    nt_dims = (((1,), (1,)), ((), ()))

    def row_body(a, carry):
        r = i * block_rows + a
        rs = jnp.clip(r - NA_ROWS // 2, 0, rows - NA_ROWS)
        win_row = rs - (i * block_rows - HALO_ROWS)
        first_off = rs - r + (NA_ROWS - 1)
        tok0 = pl.multiple_of(a * GRID_W, GRID_W)
        key0 = pl.multiple_of(win_row * GRID_W, GRID_W)
        for st in range(N_STACKS):
            lanes = slice(st * STACK_W, (st + 1) * STACK_W)
            q4 = q_ref[pl.ds(tok0, GRID_W), lanes]
            qm = jnp.concatenate(
                [jnp.where(lane_head == h, q4, jnp.zeros_like(q4)) for h in range(HEADS_PER_STACK)],
                axis=0)
            kw = kwin[pl.ds(key0, NA_ROWS * GRID_W), lanes]
            vw = vwin[pl.ds(key0, NA_ROWS * GRID_W), lanes]
            s = lax.dot_general(qm, kw, nt_dims, preferred_element_type=f32)
            sm = lax.dot_general(qm, km_ref[:, lanes], nt_dims, preferred_element_type=f32)
            bias = jnp.concatenate(
                [bias_ref[st, first_off + 2 * t] for t in range(NA_ROWS // 2)], axis=1)
            s = s + bias
            sm = sm + biasm_ref[st]
            m = jnp.maximum(jnp.max(s, axis=1, keepdims=True), jnp.max(sm, axis=1, keepdims=True))
            p = jnp.exp(s - m)
            pm = jnp.exp(sm - m)
            denom = jnp.sum(p, axis=1, keepdims=True) + jnp.sum(pm, axis=1, keepdims=True)
            o = jnp.dot(p.astype(bf16), vw, preferred_element_type=f32)
            o = o + jnp.dot(pm.astype(bf16), vm_ref[:, lanes], preferred_element_type=f32)
            o = o * (1.0 / denom)
            na = o[0:GRID_W]
            for h in range(1, HEADS_PER_STACK):
                na = jnp.where(lane_head == h, o[h * GRID_W:(h + 1) * GRID_W], na)
            gate = gn_ref[pl.ds(tok0, GRID_W), lanes].astype(f32)
            mix[pl.ds(tok0, GRID_W), D_POOL + st * STACK_W:D_POOL + (st + 1) * STACK_W] = (
                na * (gate * _sigmoid(gate))).astype(bf16)
        return carry

    lax.fori_loop(0, block_rows, row_body, 0)

    y = x_ref[...] + jnp.dot(mix[...], wout_ref[...], preferred_element_type=f32)
    ms = jnp.mean(y * y, axis=-1, keepdims=True)
    o_ref[...] = y * lax.rsqrt(ms + RMS_EPS) * fg_ref[...]


def _mixer(x, q, k, v, u, gp, gn, u_meta, k_meta, v_meta, bias, bias_meta,
           w_pool, pool_scale, w_out, final_g, block_rows):
    batch, seq_tokens, _ = x.shape
    rows = seq_tokens // GRID_W
    assert rows * GRID_W == seq_tokens and rows >= NA_ROWS
    assert rows % block_rows == 0 and block_rows % HALO_ROWS == 0 and block_rows >= NA_ROWS
    blk_tokens = block_rows * GRID_W
    halo_tokens = HALO_ROWS * GRID_W
    halo_per_blk = block_rows // HALO_ROWS
    n_halo_blocks = rows // HALO_ROWS

    def main_spec(width):
        return pl.BlockSpec((None, blk_tokens, width), lambda b, i: (b, i, 0))

    prev_spec = pl.BlockSpec((None, halo_tokens, D_NA),
                             lambda b, i: (b, jnp.maximum(i * halo_per_blk - 1, 0), 0))
    next_spec = pl.BlockSpec((None, halo_tokens, D_NA),
                             lambda b, i: (b, jnp.minimum((i + 1) * halo_per_blk, n_halo_blocks - 1), 0))

    def const_spec(shape):
        zeros = (0,) * len(shape)
        return pl.BlockSpec(shape, lambda b, i: zeros, pipeline_mode=pl.Buffered(1))

    kernel = functools.partial(_mixer_kernel, rows=rows, block_rows=block_rows)
    return pl.pallas_call(
        kernel,
        grid=(batch, rows // block_rows),
        in_specs=[
            main_spec(D_NA),
            prev_spec, main_spec(D_NA), next_spec,
            prev_spec, main_spec(D_NA), next_spec,
            prev_spec, main_spec(D_POOL), next_spec,
            main_spec(D_POOL),
            main_spec(D_NA),
            main_spec(D_MODEL),
            const_spec(u_meta.shape), const_spec(k_meta.shape), const_spec(v_meta.shape),
            const_spec(bias.shape), const_spec(bias_meta.shape),
            const_spec(w_pool.shape), const_spec(pool_scale.shape),
            const_spec(w_out.shape), const_spec(final_g.shape),
        ],
        out_specs=main_spec(D_MODEL),
        out_shape=jax.ShapeDtypeStruct(x.shape, x.dtype),
        scratch_shapes=[
            pltpu.VMEM((blk_tokens + 2 * halo_tokens, D_NA), jnp.bfloat16),
            pltpu.VMEM((blk_tokens + 2 * halo_tokens, D_NA), jnp.bfloat16),
            pltpu.VMEM((blk_tokens + 2 * POOL_HALO, D_POOL), jnp.float32),
            pltpu.VMEM((blk_tokens, D_POOL), jnp.bfloat16),
            pltpu.VMEM((blk_tokens, D_MIX), jnp.bfloat16),
        ],
        compiler_params=pltpu.CompilerParams(
            dimension_semantics=("arbitrary", "arbitrary"), vmem_limit_bytes=VMEM_LIMIT_BYTES),
        name="mixer",
    )(q, k, k, k, v, v, v, u, u, u, gp, gn, x, u_meta, k_meta, v_meta, bias, bias_meta,
      w_pool, pool_scale, w_out, final_g)


def _bias_tables(rpb, meta_bias):
    qc = jnp.arange(GRID_W)[:, None]
    kc = jnp.arange(GRID_W)[None, :]
    cstart = jnp.clip(qc - NA_COLS // 2, 0, GRID_W - NA_COLS)
    valid = (kc >= cstart) & (kc < cstart + NA_COLS)
    coloff = jnp.clip(kc - qc + (NA_COLS - 1), 0, 2 * NA_COLS - 2)
    dense = jnp.where(valid, rpb[:, :, coloff], MASKED)
    pairs = jnp.concatenate([dense[:, :-1], dense[:, 1:]], axis=-1)
    pairs = pairs.reshape(N_STACKS, HEADS_PER_STACK, PAIR_OFFSETS, GRID_W, 2 * GRID_W)
    bias = pairs.transpose(0, 2, 1, 3, 4).reshape(
        N_STACKS, PAIR_OFFSETS, HEADS_PER_STACK * GRID_W, 2 * GRID_W)
    mb = jnp.full((N_HEADS, META_PAD), MASKED, jnp.float32).at[:, :N_META].set(meta_bias)
    bias_meta = jnp.broadcast_to(mb[:, None, :], (N_HEADS, GRID_W, META_PAD)).reshape(
        N_STACKS, HEADS_PER_STACK * GRID_W, META_PAD)
    return bias.astype(jnp.float32), bias_meta


def _block_rows(rows):
    return 8


def kernel(x_prompt, x_sample, meta_tokens, norm_g, w_in, w_pool, pool_scale, rpb, meta_bias,
           w_out, final_g):
    assert norm_g.shape[0] == 1, "single-layer block"
    bf16 = jnp.bfloat16
    w_in_b = w_in[0].astype(bf16)
    w_pool_b = w_pool[0].astype(bf16)
    w_out_b = w_out[0].astype(bf16)
    g_in = norm_g[0][None, :]
    fg = final_g[None, :]
    pscale = pool_scale[0][None, :]
    bias, bias_meta = _bias_tables(rpb[0], meta_bias[0])

    u_m, _, _, k_m, v_m, _ = _in_proj(meta_tokens, g_in, w_in_b, N_META)
    pad = ((0, META_PAD - N_META), (0, 0))
    k_m = jnp.pad(k_m, pad)
    v_m = jnp.pad(v_m, pad)

    outs = []
    for x in (x_prompt, x_sample):
        batch, seq_tokens, _ = x.shape
        parts = _in_proj(x.reshape(batch * seq_tokens, D_MODEL), g_in, w_in_b, 512)
        u, gp, q, k, v, gn = [p.reshape(batch, seq_tokens, -1) for p in parts]
        outs.append(_mixer(x, q, k, v, u, gp, gn, u_m, k_m, v_m, bias, bias_meta,
                           w_pool_b, pscale, w_out_b, fg, _block_rows(seq_tokens // GRID_W)))
    return tuple(outs)
```

```python
import functools

import jax
import jax.numpy as jnp
from jax import lax
from jax.experimental import pallas as pl
from jax.experimental.pallas import tpu as pltpu

D_MODEL = 1024
N_META = 16
GRID_W = 64
D_POOL = 1024
D_NA = 1024
D_MIX = D_POOL + D_NA
D_IN = 2 * D_POOL + 4 * D_NA
POOL_WINDOWS = (2, 4, 8, 16)
POOL_GROUP_W = D_POOL // len(POOL_WINDOWS)
HEAD_DIM = 64
N_HEADS = D_NA // HEAD_DIM
NA_ROWS = 8
NA_COLS = 16
RMS_EPS = 1e-6
MASKED = -1e30

LANES = 128
BF16_ROWS = 16
MXU_WIDTH = 256
VMEM_LIMIT_BYTES = 60 * 1024 * 1024

HEADS_PER_STACK = MXU_WIDTH // HEAD_DIM
STACK_W = HEADS_PER_STACK * HEAD_DIM
N_STACKS = N_HEADS // HEADS_PER_STACK
META_PAD = LANES
HALO_ROWS = NA_ROWS // 2
POOL_CHUNK = LANES
POOL_PAD = (MXU_WIDTH - POOL_CHUNK) // 2
PAIR_OFFSETS = 2 * NA_ROWS - 2
SCORE_W = NA_ROWS * GRID_W + META_PAD


def _silu(x):
    h = 0.5 * x
    return h + h * jnp.tanh(h)


def _in_proj_kernel(x_ref, g_ref, w_ref, u_ref, gp_ref, q_ref, k_ref, v_ref, gn_ref):
    x = x_ref[...]
    ms = jnp.mean(x * x, axis=-1, keepdims=True)
    h = (x * lax.rsqrt(ms + RMS_EPS) * g_ref[...]).astype(jnp.bfloat16)
    outs = (u_ref, gp_ref, q_ref, k_ref, v_ref, gn_ref)
    for n, o_ref in enumerate(outs):
        width = o_ref.shape[-1]
        acc = jnp.dot(h, w_ref[:, n * width:(n + 1) * width], preferred_element_type=jnp.float32)
        if o_ref is q_ref:
            acc = acc * (HEAD_DIM ** -0.5)
        o_ref[...] = acc.astype(o_ref.dtype)


def _in_proj(x2d, norm_g, w_in_bf16, block_tokens):
    tokens = x2d.shape[0]
    assert tokens % block_tokens == 0
    out = jax.ShapeDtypeStruct((tokens, D_POOL), jnp.bfloat16)
    row_spec = pl.BlockSpec((block_tokens, D_MODEL), lambda i: (i, 0))
    return pl.pallas_call(
        _in_proj_kernel,
        grid=(tokens // block_tokens,),
        in_specs=[
            row_spec,
            pl.BlockSpec((1, D_MODEL), lambda i: (0, 0)),
            pl.BlockSpec((D_MODEL, D_IN), lambda i: (0, 0), pipeline_mode=pl.Buffered(1)),
        ],
        out_specs=[row_spec] * 6,
        out_shape=[out] * 6,
        compiler_params=pltpu.CompilerParams(
            dimension_semantics=("arbitrary",), vmem_limit_bytes=VMEM_LIMIT_BYTES),
        name="in_proj",
    )(x2d, norm_g, w_in_bf16)


def _mixer_kernel(q_ref, kp_ref, k_ref, kn_ref, vp_ref, v_ref, vn_ref, up_ref, u_ref, un_ref,
                  gp_ref, gn_ref, x_ref, um_ref, km_ref, vm_ref, bias_ref, biasm_ref,
                  wpool_ref, pscale_ref, wout_ref, fg_ref, o_ref,
                  kwin, vwin, ubuf, pooled, mix, s_even, s_odd, m_even, m_odd, p_even, p_odd,
                  *, rows, block_rows):
    f32, bf16 = jnp.float32, jnp.bfloat16
    blk_tokens = block_rows * GRID_W
    halo_tokens = HALO_ROWS * GRID_W
    seq_tokens = rows * GRID_W
    i = pl.program_id(1)
    is_first = i == 0
    is_last = i == pl.num_programs(1) - 1

    for win, prev, main, nxt in ((kwin, kp_ref, k_ref, kn_ref), (vwin, vp_ref, v_ref, vn_ref)):
        win[0:halo_tokens] = prev[...]
        win[halo_tokens:halo_tokens + blk_tokens] = main[...]
        win[halo_tokens + blk_tokens:] = nxt[...]

    ubuf[POOL_PAD:POOL_PAD + blk_tokens] = u_ref[...]

    @pl.when(is_first)
    def _():
        ubuf[0:POOL_PAD - N_META] = jnp.zeros((POOL_PAD - N_META, D_POOL), bf16)
        ubuf[POOL_PAD - N_META:POOL_PAD] = um_ref[...]

    @pl.when(jnp.logical_not(is_first))
    def _():
        ubuf[0:POOL_PAD] = up_ref[halo_tokens - POOL_PAD:halo_tokens, :]

    @pl.when(is_last)
    def _():
        ubuf[POOL_PAD + blk_tokens:] = jnp.zeros((POOL_PAD, D_POOL), bf16)

    @pl.when(jnp.logical_not(is_last))
    def _():
        ubuf[POOL_PAD + blk_tokens:] = un_ref[0:POOL_PAD, :]

    ctx_w = POOL_CHUNK + 2 * POOL_PAD
    out_tok = lax.broadcasted_iota(jnp.int32, (POOL_CHUNK, ctx_w), 0)
    ctx_tok = lax.broadcasted_iota(jnp.int32, (POOL_CHUNK, ctx_w), 1) - POOL_PAD
    bands = [((ctx_tok >= out_tok - w // 2) & (ctx_tok < out_tok + w // 2)).astype(bf16)
             for w in POOL_WINDOWS]
    for c in range(blk_tokens // POOL_CHUNK):
        t0 = c * POOL_CHUNK
        tok = i * blk_tokens + t0 + lax.broadcasted_iota(jnp.int32, (POOL_CHUNK, 1), 0)
        for g, w in enumerate(POOL_WINDOWS):
            lanes = slice(g * POOL_GROUP_W, (g + 1) * POOL_GROUP_W)
            win_sum = jnp.dot(bands[g], ubuf[t0:t0 + ctx_w, lanes], preferred_element_type=f32)
            inv_count = 1.0 / jnp.minimum(w, seq_tokens - tok + w // 2).astype(f32)
            centre = ubuf[POOL_PAD + t0:POOL_PAD + t0 + POOL_CHUNK, lanes].astype(f32)
            pooled[t0:t0 + POOL_CHUNK, lanes] = (win_sum * inv_count - centre).astype(bf16)

    for g in range(len(POOL_WINDOWS)):
        lanes = slice(g * POOL_GROUP_W, (g + 1) * POOL_GROUP_W)
        mixed = jnp.dot(pooled[:, lanes], wpool_ref[g], preferred_element_type=f32)
        mix[:, lanes] = (mixed * pscale_ref[:, lanes] * _silu(gp_ref[:, lanes].astype(f32))).astype(bf16)

    lane_head = lax.broadcasted_iota(jnp.int32, (1, STACK_W), 1) // HEAD_DIM
    nt_dims = (((1,), (1,)), ((), ()))
    n_keys = NA_ROWS * GRID_W

    def window_start(a):
        r = i * block_rows + a
        rs = jnp.clip(r - NA_ROWS // 2, 0, rows - NA_ROWS)
        key0 = pl.multiple_of((rs - (i * block_rows - HALO_ROWS)) * GRID_W, GRID_W)
        return key0, rs - r + (NA_ROWS - 1)

    def score_stage(a, s_buf, m_buf):
        key0, first_off = window_start(a)
        tok0 = pl.multiple_of(a * GRID_W, GRID_W)
        for st in range(N_STACKS):
            lanes = slice(st * STACK_W, (st + 1) * STACK_W)
            q4 = q_ref[pl.ds(tok0, GRID_W), lanes]
            qm = jnp.concatenate(
                [jnp.where(lane_head == h, q4, jnp.zeros_like(q4)) for h in range(HEADS_PER_STACK)],
                axis=0)
            s = lax.dot_general(qm, kwin[pl.ds(key0, n_keys), lanes], nt_dims,
                                preferred_element_type=f32)
            sm = lax.dot_general(qm, km_ref[:, lanes], nt_dims, preferred_element_type=f32)
            tile_max = sm + biasm_ref[st]
            s_buf[st, :, n_keys:] = tile_max
            for t in range(n_keys // LANES):
                cols = slice(t * LANES, (t + 1) * LANES)
                tile = s[:, cols] + bias_ref[st, first_off + 2 * t]
                s_buf[st, :, cols] = tile
                tile_max = jnp.maximum(tile_max, tile)
            m_buf[st] = jnp.broadcast_to(jnp.max(tile_max, axis=1, keepdims=True), tile_max.shape)

    def value_stage(a, s_buf, m_buf, p_buf):
        key0, _ = window_start(a)
        tok0 = pl.multiple_of(a * GRID_W, GRID_W)
        for st in range(N_STACKS):
            lanes = slice(st * STACK_W, (st + 1) * STACK_W)
            partial = []
            for blk in range(HEADS_PER_STACK * GRID_W // BF16_ROWS):
                slab = slice(blk * BF16_ROWS, (blk + 1) * BF16_ROWS)
                m = m_buf[st, slab, :]
                acc = None
                for t in range(SCORE_W // LANES):
                    cols = slice(t * LANES, (t + 1) * LANES)
                    e = jnp.exp(s_buf[st, slab, cols] - m)
                    p_buf[st, slab, cols] = e.astype(bf16)
                    acc = e if acc is None else acc + e
                partial.append(acc)
            denom = jnp.sum(jnp.concatenate(partial, axis=0), axis=1, keepdims=True)
            o = jnp.dot(p_buf[st, :, :n_keys], vwin[pl.ds(key0, n_keys), lanes],
                        preferred_element_type=f32)
            o = o + jnp.dot(p_buf[st, :, n_keys:], vm_ref[:, lanes], preferred_element_type=f32)
            o = o * (1.0 / denom)
            na = o[0:GRID_W]
            for h in range(1, HEADS_PER_STACK):
                na = jnp.where(lane_head == h, o[h * GRID_W:(h + 1) * GRID_W], na)
            mix[pl.ds(tok0, GRID_W), D_POOL + st * STACK_W:D_POOL + (st + 1) * STACK_W] = (
                na * _silu(gn_ref[pl.ds(tok0, GRID_W), lanes].astype(f32))).astype(bf16)

    score_stage(0, s_even, m_even)

    def row_pair(j, carry):
        a = 2 * j
        score_stage(a + 1, s_odd, m_odd)
        value_stage(a, s_even, m_even, p_even)
        score_stage(jnp.minimum(a + 2, block_rows - 1), s_even, m_even)
        value_stage(a + 1, s_odd, m_odd, p_odd)
        return carry

    lax.fori_loop(0, block_rows // 2, row_pair, 0)

    y = x_ref[...] + jnp.dot(mix[...], wout_ref[...], preferred_element_type=f32)
    ms = jnp.mean(y * y, axis=-1, keepdims=True)
    o_ref[...] = y * lax.rsqrt(ms + RMS_EPS) * fg_ref[...]


def _mixer(x, q, k, v, u, gp, gn, u_meta, k_meta, v_meta, bias, bias_meta,
           w_pool, pool_scale, w_out, final_g, block_rows):
    batch, seq_tokens, _ = x.shape
    rows = seq_tokens // GRID_W
    assert rows * GRID_W == seq_tokens and rows >= NA_ROWS
    assert rows % block_rows == 0 and block_rows % HALO_ROWS == 0 and block_rows >= NA_ROWS
    assert block_rows % 2 == 0 and (block_rows * GRID_W) % POOL_CHUNK == 0
    blk_tokens = block_rows * GRID_W
    halo_tokens = HALO_ROWS * GRID_W
    halo_per_blk = block_rows // HALO_ROWS
    n_halo_blocks = rows // HALO_ROWS

    def main_spec(width):
        return pl.BlockSpec((None, blk_tokens, width), lambda b, i: (b, i, 0))

    prev_spec = pl.BlockSpec((None, halo_tokens, D_NA),
                             lambda b, i: (b, jnp.maximum(i * halo_per_blk - 1, 0), 0))
    next_spec = pl.BlockSpec((None, halo_tokens, D_NA),
                             lambda b, i: (b, jnp.minimum((i + 1) * halo_per_blk, n_halo_blocks - 1), 0))

    def const_spec(shape):
        zeros = (0,) * len(shape)
        return pl.BlockSpec(shape, lambda b, i: zeros, pipeline_mode=pl.Buffered(1))

    stack_rows = HEADS_PER_STACK * GRID_W
    score_buf = pltpu.VMEM((N_STACKS, stack_rows, SCORE_W), jnp.float32)
    max_buf = pltpu.VMEM((N_STACKS, stack_rows, LANES), jnp.float32)
    prob_buf = pltpu.VMEM((N_STACKS, stack_rows, SCORE_W), jnp.bfloat16)
    kernel = functools.partial(_mixer_kernel, rows=rows, block_rows=block_rows)
    return pl.pallas_call(
        kernel,
        grid=(batch, rows // block_rows),
        in_specs=[
            main_spec(D_NA),
            prev_spec, main_spec(D_NA), next_spec,
            prev_spec, main_spec(D_NA), next_spec,
            prev_spec, main_spec(D_POOL), next_spec,
            main_spec(D_POOL),
            main_spec(D_NA),
            main_spec(D_MODEL),
            const_spec(u_meta.shape), const_spec(k_meta.shape), const_spec(v_meta.shape),
            const_spec(bias.shape), const_spec(bias_meta.shape),
            const_spec(w_pool.shape), const_spec(pool_scale.shape),
            const_spec(w_out.shape), const_spec(final_g.shape),
        ],
        out_specs=main_spec(D_MODEL),
        out_shape=jax.ShapeDtypeStruct(x.shape, x.dtype),
        scratch_shapes=[
            pltpu.VMEM((blk_tokens + 2 * halo_tokens, D_NA), jnp.bfloat16),
            pltpu.VMEM((blk_tokens + 2 * halo_tokens, D_NA), jnp.bfloat16),
            pltpu.VMEM((blk_tokens + 2 * POOL_PAD, D_POOL), jnp.bfloat16),
            pltpu.VMEM((blk_tokens, D_POOL), jnp.bfloat16),
            pltpu.VMEM((blk_tokens, D_MIX), jnp.bfloat16),
            score_buf, score_buf,
            max_buf, max_buf,
            prob_buf, prob_buf,
        ],
        compiler_params=pltpu.CompilerParams(
            dimension_semantics=("arbitrary", "arbitrary"), vmem_limit_bytes=VMEM_LIMIT_BYTES),
        name="mixer",
    )(q, k, k, k, v, v, v, u, u, u, gp, gn, x, u_meta, k_meta, v_meta, bias, bias_meta,
      w_pool, pool_scale, w_out, final_g)


def _bias_tables(rpb, meta_bias):
    qc = jnp.arange(GRID_W)[:, None]
    kc = jnp.arange(GRID_W)[None, :]
    cstart = jnp.clip(qc - NA_COLS // 2, 0, GRID_W - NA_COLS)
    valid = (kc >= cstart) & (kc < cstart + NA_COLS)
    coloff = jnp.clip(kc - qc + (NA_COLS - 1), 0, 2 * NA_COLS - 2)
    dense = jnp.where(valid, rpb[:, :, coloff], MASKED)
    pairs = jnp.concatenate([dense[:, :-1], dense[:, 1:]], axis=-1)
    pairs = pairs.reshape(N_STACKS, HEADS_PER_STACK, PAIR_OFFSETS, GRID_W, 2 * GRID_W)
    bias = pairs.transpose(0, 2, 1, 3, 4).reshape(
        N_STACKS, PAIR_OFFSETS, HEADS_PER_STACK * GRID_W, 2 * GRID_W)
    mb = jnp.full((N_HEADS, META_PAD), MASKED, jnp.float32).at[:, :N_META].set(meta_bias)
    bias_meta = jnp.broadcast_to(mb[:, None, :], (N_HEADS, GRID_W, META_PAD)).reshape(
        N_STACKS, HEADS_PER_STACK * GRID_W, META_PAD)
    return bias.astype(jnp.float32), bias_meta


def _block_rows(rows):
    return 8


def kernel(x_prompt, x_sample, meta_tokens, norm_g, w_in, w_pool, pool_scale, rpb, meta_bias,
           w_out, final_g):
    assert norm_g.shape[0] == 1, "single-layer block"
    bf16 = jnp.bfloat16
    w_in_b = w_in[0].astype(bf16)
    w_pool_b = w_pool[0].astype(bf16)
    w_out_b = w_out[0].astype(bf16)
    g_in = norm_g[0][None, :]
    fg = final_g[None, :]
    pscale = pool_scale[0][None, :]
    bias, bias_meta = _bias_tables(rpb[0], meta_bias[0])

    u_m, _, _, k_m, v_m, _ = _in_proj(meta_tokens, g_in, w_in_b, N_META)
    pad = ((0, META_PAD - N_META), (0, 0))
    k_m = jnp.pad(k_m, pad)
    v_m = jnp.pad(v_m, pad)

    outs = []
    for x in (x_prompt, x_sample):
        batch, seq_tokens, _ = x.shape
        parts = _in_proj(x.reshape(batch * seq_tokens, D_MODEL), g_in, w_in_b, 512)
        u, gp, q, k, v, gn = [p.reshape(batch, seq_tokens, -1) for p in parts]
        outs.append(_mixer(x, q, k, v, u, gp, gn, u_m, k_m, v_m, bias, bias_meta,
                           w_pool_b, pscale, w_out_b, fg, _block_rows(seq_tokens // GRID_W)))
    return tuple(outs)
```

```python
import functools

import jax
import jax.numpy as jnp
from jax import lax
from jax.experimental import pallas as pl
from jax.experimental.pallas import tpu as pltpu

D_MODEL = 1024
N_META = 16
GRID_W = 64
D_POOL = 1024
D_NA = 1024
D_MIX = D_POOL + D_NA
D_IN = 2 * D_POOL + 4 * D_NA
POOL_WINDOWS = (2, 4, 8, 16)
POOL_GROUP_W = D_POOL // len(POOL_WINDOWS)
HEAD_DIM = 64
N_HEADS = D_NA // HEAD_DIM
NA_ROWS = 8
NA_COLS = 16
RMS_EPS = 1e-6
MASKED = -1e30

LANES = 128
BF16_ROWS = 16
MXU_WIDTH = 256
VMEM_LIMIT_BYTES = 60 * 1024 * 1024

HEADS_PER_STACK = MXU_WIDTH // HEAD_DIM
STACK_W = HEADS_PER_STACK * HEAD_DIM
N_STACKS = N_HEADS // HEADS_PER_STACK
META_PAD = LANES
HALO_ROWS = NA_ROWS // 2
POOL_CHUNK = LANES
POOL_PAD = (MXU_WIDTH - POOL_CHUNK) // 2
PAIR_OFFSETS = 2 * NA_ROWS - 2
SCORE_W = NA_ROWS * GRID_W + META_PAD
IN_PROJ_TOKENS = 1024


def _silu(x):
    h = 0.5 * x
    return h + h * jnp.tanh(h)


def _in_proj_kernel(x_ref, g_ref, w_ref, u_ref, gp_ref, q_ref, k_ref, v_ref, gn_ref):
    x = x_ref[...]
    ms = jnp.mean(x * x, axis=-1, keepdims=True)
    h = (x * lax.rsqrt(ms + RMS_EPS) * g_ref[...]).astype(jnp.bfloat16)
    outs = (u_ref, gp_ref, q_ref, k_ref, v_ref, gn_ref)
    for n, o_ref in enumerate(outs):
        width = o_ref.shape[-1]
        acc = jnp.dot(h, w_ref[:, n * width:(n + 1) * width], preferred_element_type=jnp.float32)
        if o_ref is q_ref:
            acc = acc * (HEAD_DIM ** -0.5)
        o_ref[...] = acc.astype(o_ref.dtype)


def _in_proj(x2d, norm_g, w_in_bf16, block_tokens):
    tokens = x2d.shape[0]
    assert tokens % block_tokens == 0
    out = jax.ShapeDtypeStruct((tokens, D_POOL), jnp.bfloat16)
    row_spec = pl.BlockSpec((block_tokens, D_MODEL), lambda i: (i, 0))
    return pl.pallas_call(
        _in_proj_kernel,
        grid=(tokens // block_tokens,),
        in_specs=[
            row_spec,
            pl.BlockSpec((1, D_MODEL), lambda i: (0, 0)),
            pl.BlockSpec((D_MODEL, D_IN), lambda i: (0, 0), pipeline_mode=pl.Buffered(1)),
        ],
        out_specs=[row_spec] * 6,
        out_shape=[out] * 6,
        compiler_params=pltpu.CompilerParams(
            dimension_semantics=("arbitrary",), vmem_limit_bytes=VMEM_LIMIT_BYTES),
        name="in_proj",
    )(x2d, norm_g, w_in_bf16)


def _mixer_kernel(q_ref, kp_ref, k_ref, kn_ref, vp_ref, v_ref, vn_ref, up_ref, u_ref, un_ref,
                  gp_ref, gn_ref, x_ref, um_ref, km_ref, vm_ref, bias_ref, biasm_ref,
                  wpool_ref, pscale_ref, wout_ref, fg_ref, o_ref,
                  kwin, vwin, ubuf, pooled, mix, s_even, s_odd, m_even, m_odd, p_even, p_odd,
                  *, rows, block_rows):
    f32, bf16 = jnp.float32, jnp.bfloat16
    blk_tokens = block_rows * GRID_W
    halo_tokens = HALO_ROWS * GRID_W
    seq_tokens = rows * GRID_W
    i = pl.program_id(1)
    is_first = i == 0
    is_last = i == pl.num_programs(1) - 1

    for win, prev, main, nxt in ((kwin, kp_ref, k_ref, kn_ref), (vwin, vp_ref, v_ref, vn_ref)):
        win[0:halo_tokens] = prev[...]
        win[halo_tokens:halo_tokens + blk_tokens] = main[...]
        win[halo_tokens + blk_tokens:] = nxt[...]

    ubuf[POOL_PAD:POOL_PAD + blk_tokens] = u_ref[...]

    @pl.when(is_first)
    def _():
        ubuf[0:POOL_PAD - N_META] = jnp.zeros((POOL_PAD - N_META, D_POOL), bf16)
        ubuf[POOL_PAD - N_META:POOL_PAD] = um_ref[...]

    @pl.when(jnp.logical_not(is_first))
    def _():
        ubuf[0:POOL_PAD] = up_ref[halo_tokens - POOL_PAD:halo_tokens, :]

    @pl.when(is_last)
    def _():
        ubuf[POOL_PAD + blk_tokens:] = jnp.zeros((POOL_PAD, D_POOL), bf16)

    @pl.when(jnp.logical_not(is_last))
    def _():
        ubuf[POOL_PAD + blk_tokens:] = un_ref[0:POOL_PAD, :]

    ctx_w = POOL_CHUNK + 2 * POOL_PAD
    out_tok = lax.broadcasted_iota(jnp.int32, (POOL_CHUNK, ctx_w), 0)
    ctx_tok = lax.broadcasted_iota(jnp.int32, (POOL_CHUNK, ctx_w), 1) - POOL_PAD
    bands = [((ctx_tok >= out_tok - w // 2) & (ctx_tok < out_tok + w // 2)).astype(bf16)
             for w in POOL_WINDOWS]
    for c in range(blk_tokens // POOL_CHUNK):
        t0 = c * POOL_CHUNK
        tok = i * blk_tokens + t0 + lax.broadcasted_iota(jnp.int32, (POOL_CHUNK, 1), 0)
        for g, w in enumerate(POOL_WINDOWS):
            lanes = slice(g * POOL_GROUP_W, (g + 1) * POOL_GROUP_W)
            win_sum = jnp.dot(bands[g], ubuf[t0:t0 + ctx_w, lanes], preferred_element_type=f32)
            inv_count = 1.0 / jnp.minimum(w, seq_tokens - tok + w // 2).astype(f32)
            centre = ubuf[POOL_PAD + t0:POOL_PAD + t0 + POOL_CHUNK, lanes].astype(f32)
            pooled[t0:t0 + POOL_CHUNK, lanes] = (win_sum * inv_count - centre).astype(bf16)

    for g in range(len(POOL_WINDOWS)):
        lanes = slice(g * POOL_GROUP_W, (g + 1) * POOL_GROUP_W)
        mixed = jnp.dot(pooled[:, lanes], wpool_ref[g], preferred_element_type=f32)
        mix[:, lanes] = (mixed * pscale_ref[:, lanes] * _silu(gp_ref[:, lanes].astype(f32))).astype(bf16)

    lane_head = lax.broadcasted_iota(jnp.int32, (1, STACK_W), 1) // HEAD_DIM
    nt_dims = (((1,), (1,)), ((), ()))
    n_keys = NA_ROWS * GRID_W

    def window_start(a):
        r = i * block_rows + a
        rs = jnp.clip(r - NA_ROWS // 2, 0, rows - NA_ROWS)
        key0 = pl.multiple_of((rs - (i * block_rows - HALO_ROWS)) * GRID_W, GRID_W)
        return key0, rs - r + (NA_ROWS - 1)

    def score_stage(a, s_buf, m_buf):
        key0, first_off = window_start(a)
        tok0 = pl.multiple_of(a * GRID_W, GRID_W)
        for st in range(N_STACKS):
            lanes = slice(st * STACK_W, (st + 1) * STACK_W)
            q4 = q_ref[pl.ds(tok0, GRID_W), lanes]
            qm = jnp.concatenate(
                [jnp.where(lane_head == h, q4, jnp.zeros_like(q4)) for h in range(HEADS_PER_STACK)],
                axis=0)
            s = lax.dot_general(qm, kwin[pl.ds(key0, n_keys), lanes], nt_dims,
                                preferred_element_type=f32)
            sm = lax.dot_general(qm, km_ref[:, lanes], nt_dims, preferred_element_type=f32)
            tile_max = sm + biasm_ref[st]
            s_buf[st, :, n_keys:] = tile_max
            for t in range(n_keys // LANES):
                cols = slice(t * LANES, (t + 1) * LANES)
                tile = s[:, cols] + bias_ref[st, first_off + 2 * t]
                s_buf[st, :, cols] = tile
                tile_max = jnp.maximum(tile_max, tile)
            m_buf[st] = jnp.broadcast_to(jnp.max(tile_max, axis=1, keepdims=True), tile_max.shape)

    def value_stage(a, s_buf, m_buf, p_buf):
        key0, _ = window_start(a)
        tok0 = pl.multiple_of(a * GRID_W, GRID_W)
        for st in range(N_STACKS):
            lanes = slice(st * STACK_W, (st + 1) * STACK_W)
            partial = []
            for blk in range(HEADS_PER_STACK * GRID_W // BF16_ROWS):
                slab = slice(blk * BF16_ROWS, (blk + 1) * BF16_ROWS)
                m = m_buf[st, slab, :]
                acc = None
                for t in range(SCORE_W // LANES):
                    cols = slice(t * LANES, (t + 1) * LANES)
                    e = jnp.exp(s_buf[st, slab, cols] - m)
                    p_buf[st, slab, cols] = e.astype(bf16)
                    acc = e if acc is None else acc + e
                partial.append(acc)
            denom = jnp.sum(jnp.concatenate(partial, axis=0), axis=1, keepdims=True)
            o = jnp.dot(p_buf[st, :, :n_keys], vwin[pl.ds(key0, n_keys), lanes],
                        preferred_element_type=f32)
            o = o + jnp.dot(p_buf[st, :, n_keys:], vm_ref[:, lanes], preferred_element_type=f32)
            o = o * (1.0 / denom)
            na = o[0:GRID_W]
            for h in range(1, HEADS_PER_STACK):
                na = jnp.where(lane_head == h, o[h * GRID_W:(h + 1) * GRID_W], na)
            mix[pl.ds(tok0, GRID_W), D_POOL + st * STACK_W:D_POOL + (st + 1) * STACK_W] = (
                na * _silu(gn_ref[pl.ds(tok0, GRID_W), lanes].astype(f32))).astype(bf16)

    score_stage(0, s_even, m_even)

    def row_pair(j, carry):
        a = 2 * j
        score_stage(a + 1, s_odd, m_odd)
        value_stage(a, s_even, m_even, p_even)
        score_stage(jnp.minimum(a + 2, block_rows - 1), s_even, m_even)
        value_stage(a + 1, s_odd, m_odd, p_odd)
        return carry

    lax.fori_loop(0, block_rows // 2, row_pair, 0)

    y = x_ref[...] + jnp.dot(mix[...], wout_ref[...], preferred_element_type=f32)
    ms = jnp.mean(y * y, axis=-1, keepdims=True)
    o_ref[...] = y * lax.rsqrt(ms + RMS_EPS) * fg_ref[...]


def _mixer(x, q, k, v, u, gp, gn, u_meta, k_meta, v_meta, bias, bias_meta,
           w_pool, pool_scale, w_out, final_g, block_rows):
    batch, seq_tokens, _ = x.shape
    rows = seq_tokens // GRID_W
    assert rows * GRID_W == seq_tokens and rows >= NA_ROWS
    assert rows % block_rows == 0 and block_rows % HALO_ROWS == 0 and block_rows >= NA_ROWS
    assert block_rows % 2 == 0 and (block_rows * GRID_W) % POOL_CHUNK == 0
    blk_tokens = block_rows * GRID_W
    halo_tokens = HALO_ROWS * GRID_W
    halo_per_blk = block_rows // HALO_ROWS
    n_halo_blocks = rows // HALO_ROWS

    def main_spec(width):
        return pl.BlockSpec((None, blk_tokens, width), lambda b, i: (b, i, 0))

    prev_spec = pl.BlockSpec((None, halo_tokens, D_NA),
                             lambda b, i: (b, jnp.maximum(i * halo_per_blk - 1, 0), 0))
    next_spec = pl.BlockSpec((None, halo_tokens, D_NA),
                             lambda b, i: (b, jnp.minimum((i + 1) * halo_per_blk, n_halo_blocks - 1), 0))

    def const_spec(shape):
        zeros = (0,) * len(shape)
        return pl.BlockSpec(shape, lambda b, i: zeros, pipeline_mode=pl.Buffered(1))

    stack_rows = HEADS_PER_STACK * GRID_W
    score_buf = pltpu.VMEM((N_STACKS, stack_rows, SCORE_W), jnp.float32)
    max_buf = pltpu.VMEM((N_STACKS, stack_rows, LANES), jnp.float32)
    prob_buf = pltpu.VMEM((N_STACKS, stack_rows, SCORE_W), jnp.bfloat16)
    kernel = functools.partial(_mixer_kernel, rows=rows, block_rows=block_rows)
    return pl.pallas_call(
        kernel,
        grid=(batch, rows // block_rows),
        in_specs=[
            main_spec(D_NA),
            prev_spec, main_spec(D_NA), next_spec,
            prev_spec, main_spec(D_NA), next_spec,
            prev_spec, main_spec(D_POOL), next_spec,
            main_spec(D_POOL),
            main_spec(D_NA),
            main_spec(D_MODEL),
            const_spec(u_meta.shape), const_spec(k_meta.shape), const_spec(v_meta.shape),
            const_spec(bias.shape), const_spec(bias_meta.shape),
            const_spec(w_pool.shape), const_spec(pool_scale.shape),
            const_spec(w_out.shape), const_spec(final_g.shape),
        ],
        out_specs=main_spec(D_MODEL),
        out_shape=jax.ShapeDtypeStruct(x.shape, x.dtype),
        scratch_shapes=[
            pltpu.VMEM((blk_tokens + 2 * halo_tokens, D_NA), jnp.bfloat16),
            pltpu.VMEM((blk_tokens + 2 * halo_tokens, D_NA), jnp.bfloat16),
            pltpu.VMEM((blk_tokens + 2 * POOL_PAD, D_POOL), jnp.bfloat16),
            pltpu.VMEM((blk_tokens, D_POOL), jnp.bfloat16),
            pltpu.VMEM((blk_tokens, D_MIX), jnp.bfloat16),
            score_buf, score_buf,
            max_buf, max_buf,
            prob_buf, prob_buf,
        ],
        compiler_params=pltpu.CompilerParams(
            dimension_semantics=("arbitrary", "arbitrary"), vmem_limit_bytes=VMEM_LIMIT_BYTES),
        name="mixer",
    )(q, k, k, k, v, v, v, u, u, u, gp, gn, x, u_meta, k_meta, v_meta, bias, bias_meta,
      w_pool, pool_scale, w_out, final_g)


def _bias_tables(rpb, meta_bias):
    qc = jnp.arange(GRID_W)[:, None]
    kc = jnp.arange(GRID_W)[None, :]
    cstart = jnp.clip(qc - NA_COLS // 2, 0, GRID_W - NA_COLS)
    valid = (kc >= cstart) & (kc < cstart + NA_COLS)
    onehot = ((kc - qc + (NA_COLS - 1))[:, :, None] == jnp.arange(2 * NA_COLS - 1)).astype(jnp.float32)
    rpb_st = rpb.reshape(N_STACKS, HEADS_PER_STACK, 2 * NA_ROWS - 1, 2 * NA_COLS - 1)
    dense = jnp.einsum("qkd,shod->sohqk", onehot, rpb_st, precision=lax.Precision.HIGHEST)
    dense = jnp.where(valid, dense, MASKED)
    pairs = jnp.concatenate([dense[:, :-1], dense[:, 1:]], axis=-1)
    bias = pairs.reshape(N_STACKS, PAIR_OFFSETS, HEADS_PER_STACK * GRID_W, 2 * GRID_W)
    mb = jnp.full((N_HEADS, META_PAD), MASKED, jnp.float32).at[:, :N_META].set(meta_bias)
    bias_meta = jnp.broadcast_to(mb[:, None, :], (N_HEADS, GRID_W, META_PAD)).reshape(
        N_STACKS, HEADS_PER_STACK * GRID_W, META_PAD)
    return bias.astype(jnp.float32), bias_meta


def _block_rows(rows):
    return 8


def kernel(x_prompt, x_sample, meta_tokens, norm_g, w_in, w_pool, pool_scale, rpb, meta_bias,
           w_out, final_g):
    assert norm_g.shape[0] == 1, "single-layer block"
    bf16 = jnp.bfloat16
    w_in_b = w_in[0].astype(bf16)
    w_pool_b = w_pool[0].astype(bf16)
    w_out_b = w_out[0].astype(bf16)
    g_in = norm_g[0][None, :]
    fg = final_g[None, :]
    pscale = pool_scale[0][None, :]
    bias, bias_meta = _bias_tables(rpb[0], meta_bias[0])

    u_m, _, _, k_m, v_m, _ = _in_proj(meta_tokens, g_in, w_in_b, N_META)
    pad = ((0, META_PAD - N_META), (0, 0))
    k_m = jnp.pad(k_m, pad)
    v_m = jnp.pad(v_m, pad)

    outs = []
    for x in (x_prompt, x_sample):
        batch, seq_tokens, _ = x.shape
        parts = _in_proj(x.reshape(batch * seq_tokens, D_MODEL), g_in, w_in_b, IN_PROJ_TOKENS)
        u, gp, q, k, v, gn = [p.reshape(batch, seq_tokens, -1) for p in parts]
        outs.append(_mixer(x, q, k, v, u, gp, gn, u_m, k_m, v_m, bias, bias_meta,
                           w_pool_b, pscale, w_out_b, fg, _block_rows(seq_tokens // GRID_W)))
    return tuple(outs)
```

```python
import functools

import jax
import jax.numpy as jnp
from jax import lax
from jax.experimental import pallas as pl
from jax.experimental.pallas import tpu as pltpu

D_MODEL = 1024
N_META = 16
GRID_W = 64
D_POOL = 1024
D_NA = 1024
D_MIX = D_POOL + D_NA
D_IN = 2 * D_POOL + 4 * D_NA
POOL_WINDOWS = (2, 4, 8, 16)
POOL_GROUP_W = D_POOL // len(POOL_WINDOWS)
HEAD_DIM = 64
N_HEADS = D_NA // HEAD_DIM
NA_ROWS = 8
NA_COLS = 16
RMS_EPS = 1e-6
MASKED = -1e30

LANES = 128
BF16_ROWS = 16
MXU_WIDTH = 256
VMEM_LIMIT_BYTES = 60 * 1024 * 1024

HEADS_PER_STACK = MXU_WIDTH // HEAD_DIM
STACK_W = HEADS_PER_STACK * HEAD_DIM
N_STACKS = N_HEADS // HEADS_PER_STACK
META_PAD = LANES
HALO_ROWS = NA_ROWS // 2
POOL_CHUNK = LANES
POOL_PAD = (MXU_WIDTH - POOL_CHUNK) // 2
PAIR_OFFSETS = 2 * NA_ROWS - 2
SCORE_W = NA_ROWS * GRID_W + META_PAD
IN_PROJ_TOKENS = 1024


def _silu(x):
    h = 0.5 * x
    return h + h * jnp.tanh(h)


def _in_proj_kernel(x_ref, g_ref, w_ref, u_ref, gp_ref, q_ref, k_ref, v_ref, gn_ref):
    x = x_ref[...]
    ms = jnp.mean(x * x, axis=-1, keepdims=True)
    h = (x * lax.rsqrt(ms + RMS_EPS) * g_ref[...]).astype(jnp.bfloat16)
    outs = (u_ref, gp_ref, q_ref, k_ref, v_ref, gn_ref)
    for n, o_ref in enumerate(outs):
        width = o_ref.shape[-1]
        acc = jnp.dot(h, w_ref[:, n * width:(n + 1) * width], preferred_element_type=jnp.float32)
        if o_ref is q_ref:
            acc = acc * (HEAD_DIM ** -0.5)
        o_ref[...] = acc.astype(o_ref.dtype)


def _in_proj(x2d, norm_g, w_in_bf16, block_tokens):
    tokens = x2d.shape[0]
    assert tokens % block_tokens == 0
    out = jax.ShapeDtypeStruct((tokens, D_POOL), jnp.bfloat16)
    row_spec = pl.BlockSpec((block_tokens, D_MODEL), lambda i: (i, 0))
    return pl.pallas_call(
        _in_proj_kernel,
        grid=(tokens // block_tokens,),
        in_specs=[
            row_spec,
            pl.BlockSpec((1, D_MODEL), lambda i: (0, 0)),
            pl.BlockSpec((D_MODEL, D_IN), lambda i: (0, 0), pipeline_mode=pl.Buffered(1)),
        ],
        out_specs=[row_spec] * 6,
        out_shape=[out] * 6,
        compiler_params=pltpu.CompilerParams(
            dimension_semantics=("arbitrary",), vmem_limit_bytes=VMEM_LIMIT_BYTES),
        name="in_proj",
    )(x2d, norm_g, w_in_bf16)


def _mixer_kernel(q_ref, kwin, vwin, up_ref, u_ref, un_ref,
                  gp_ref, gn_ref, x_ref, um_ref, km_ref, vm_ref, bias_ref, biasm_ref,
                  wpool_ref, pscale_ref, wout_ref, fg_ref, o_ref,
                  ubuf, pooled, mix, s_even, s_odd, m_even, m_odd, p_even, p_odd,
                  *, rows, block_rows):
    f32, bf16 = jnp.float32, jnp.bfloat16
    blk_tokens = block_rows * GRID_W
    halo_tokens = HALO_ROWS * GRID_W
    seq_tokens = rows * GRID_W
    i = pl.program_id(1)
    is_first = i == 0
    is_last = i == pl.num_programs(1) - 1

    win_row0 = jnp.clip(i * block_rows - HALO_ROWS, 0, rows - (block_rows + 2 * HALO_ROWS))

    ubuf[POOL_PAD:POOL_PAD + blk_tokens] = u_ref[...]

    @pl.when(is_first)
    def _():
        ubuf[0:POOL_PAD - N_META] = jnp.zeros((POOL_PAD - N_META, D_POOL), bf16)
        ubuf[POOL_PAD - N_META:POOL_PAD] = um_ref[...]

    @pl.when(jnp.logical_not(is_first))
    def _():
        ubuf[0:POOL_PAD] = up_ref[halo_tokens - POOL_PAD:halo_tokens, :]

    @pl.when(is_last)
    def _():
        ubuf[POOL_PAD + blk_tokens:] = jnp.zeros((POOL_PAD, D_POOL), bf16)

    @pl.when(jnp.logical_not(is_last))
    def _():
        ubuf[POOL_PAD + blk_tokens:] = un_ref[0:POOL_PAD, :]

    ctx_w = POOL_CHUNK + 2 * POOL_PAD
    out_tok = lax.broadcasted_iota(jnp.int32, (POOL_CHUNK, ctx_w), 0)
    ctx_tok = lax.broadcasted_iota(jnp.int32, (POOL_CHUNK, ctx_w), 1) - POOL_PAD
    bands = [((ctx_tok >= out_tok - w // 2) & (ctx_tok < out_tok + w // 2)).astype(bf16)
             for w in POOL_WINDOWS]
    for c in range(blk_tokens // POOL_CHUNK):
        t0 = c * POOL_CHUNK
        tok = i * blk_tokens + t0 + lax.broadcasted_iota(jnp.int32, (POOL_CHUNK, 1), 0)
        for g, w in enumerate(POOL_WINDOWS):
            lanes = slice(g * POOL_GROUP_W, (g + 1) * POOL_GROUP_W)
            win_sum = jnp.dot(bands[g], ubuf[t0:t0 + ctx_w, lanes], preferred_element_type=f32)
            inv_count = 1.0 / jnp.minimum(w, seq_tokens - tok + w // 2).astype(f32)
            centre = ubuf[POOL_PAD + t0:POOL_PAD + t0 + POOL_CHUNK, lanes].astype(f32)
            pooled[t0:t0 + POOL_CHUNK, lanes] = (win_sum * inv_count - centre).astype(bf16)

    for g in range(len(POOL_WINDOWS)):
        lanes = slice(g * POOL_GROUP_W, (g + 1) * POOL_GROUP_W)
        mixed = jnp.dot(pooled[:, lanes], wpool_ref[g], preferred_element_type=f32)
        mix[:, lanes] = (mixed * pscale_ref[:, lanes] * _silu(gp_ref[:, lanes].astype(f32))).astype(bf16)

    lane_head = lax.broadcasted_iota(jnp.int32, (1, STACK_W), 1) // HEAD_DIM
    nt_dims = (((1,), (1,)), ((), ()))
    n_keys = NA_ROWS * GRID_W

    def window_start(a):
        r = i * block_rows + a
        rs = jnp.clip(r - NA_ROWS // 2, 0, rows - NA_ROWS)
        key0 = pl.multiple_of((rs - win_row0) * GRID_W, GRID_W)
        return key0, rs - r + (NA_ROWS - 1)

    def score_stage(a, s_buf, m_buf):
        key0, first_off = window_start(a)
        tok0 = pl.multiple_of(a * GRID_W, GRID_W)
        for st in range(N_STACKS):
            lanes = slice(st * STACK_W, (st + 1) * STACK_W)
            q4 = q_ref[pl.ds(tok0, GRID_W), lanes]
            qm = jnp.concatenate(
                [jnp.where(lane_head == h, q4, jnp.zeros_like(q4)) for h in range(HEADS_PER_STACK)],
                axis=0)
            s = lax.dot_general(qm, kwin[0, pl.ds(key0, n_keys), lanes], nt_dims,
                                preferred_element_type=f32)
            sm = lax.dot_general(qm, km_ref[:, lanes], nt_dims, preferred_element_type=f32)
            tile_max = sm + biasm_ref[st]
            s_buf[st, :, n_keys:] = tile_max
            for t in range(n_keys // LANES):
                cols = slice(t * LANES, (t + 1) * LANES)
                tile = s[:, cols] + bias_ref[st, first_off + 2 * t]
                s_buf[st, :, cols] = tile
                tile_max = jnp.maximum(tile_max, tile)
            m_buf[st] = jnp.broadcast_to(jnp.max(tile_max, axis=1, keepdims=True), tile_max.shape)

    def value_stage(a, s_buf, m_buf, p_buf):
        key0, _ = window_start(a)
        tok0 = pl.multiple_of(a * GRID_W, GRID_W)
        for st in range(N_STACKS):
            lanes = slice(st * STACK_W, (st + 1) * STACK_W)
            partial = []
            for blk in range(HEADS_PER_STACK * GRID_W // BF16_ROWS):
                slab = slice(blk * BF16_ROWS, (blk + 1) * BF16_ROWS)
                m = m_buf[st, slab, :]
                acc = None
                for t in range(SCORE_W // LANES):
                    cols = slice(t * LANES, (t + 1) * LANES)
                    e = jnp.exp(s_buf[st, slab, cols] - m)
                    p_buf[st, slab, cols] = e.astype(bf16)
                    acc = e if acc is None else acc + e
                partial.append(acc)
            denom = jnp.sum(jnp.concatenate(partial, axis=0), axis=1, keepdims=True)
            o = jnp.dot(p_buf[st, :, :n_keys], vwin[0, pl.ds(key0, n_keys), lanes],
                        preferred_element_type=f32)
            o = o + jnp.dot(p_buf[st, :, n_keys:], vm_ref[:, lanes], preferred_element_type=f32)
            o = o * (1.0 / denom)
            na = o[0:GRID_W]
            for h in range(1, HEADS_PER_STACK):
                na = jnp.where(lane_head == h, o[h * GRID_W:(h + 1) * GRID_W], na)
            mix[pl.ds(tok0, GRID_W), D_POOL + st * STACK_W:D_POOL + (st + 1) * STACK_W] = (
                na * _silu(gn_ref[pl.ds(tok0, GRID_W), lanes].astype(f32))).astype(bf16)

    score_stage(0, s_even, m_even)

    def row_pair(j, carry):
        a = 2 * j
        score_stage(a + 1, s_odd, m_odd)
        value_stage(a, s_even, m_even, p_even)
        score_stage(jnp.minimum(a + 2, block_rows - 1), s_even, m_even)
        value_stage(a + 1, s_odd, m_odd, p_odd)
        return carry

    lax.fori_loop(0, block_rows // 2, row_pair, 0)

    y = x_ref[...] + jnp.dot(mix[...], wout_ref[...], preferred_element_type=f32)
    ms = jnp.mean(y * y, axis=-1, keepdims=True)
    o_ref[...] = y * lax.rsqrt(ms + RMS_EPS) * fg_ref[...]


def _mixer(x, q, k, v, u, gp, gn, u_meta, k_meta, v_meta, bias, bias_meta,
           w_pool, pool_scale, w_out, final_g, block_rows):
    batch, seq_tokens, _ = x.shape
    rows = seq_tokens // GRID_W
    assert rows * GRID_W == seq_tokens and rows >= NA_ROWS
    assert rows % block_rows == 0 and block_rows % HALO_ROWS == 0 and block_rows >= NA_ROWS
    assert block_rows % 2 == 0 and (block_rows * GRID_W) % POOL_CHUNK == 0
    blk_tokens = block_rows * GRID_W
    halo_tokens = HALO_ROWS * GRID_W
    halo_per_blk = block_rows // HALO_ROWS
    n_halo_blocks = rows // HALO_ROWS

    def main_spec(width):
        return pl.BlockSpec((None, blk_tokens, width), lambda b, i: (b, i, 0))

    prev_spec = pl.BlockSpec((None, halo_tokens, D_NA),
                             lambda b, i: (b, jnp.maximum(i * halo_per_blk - 1, 0), 0))
    next_spec = pl.BlockSpec((None, halo_tokens, D_NA),
                             lambda b, i: (b, jnp.minimum((i + 1) * halo_per_blk, n_halo_blocks - 1), 0))

    win_tokens = blk_tokens + 2 * halo_tokens
    assert seq_tokens >= win_tokens
    win_spec = pl.BlockSpec(
        (pl.Element(1), pl.Element(win_tokens), pl.Element(D_NA)),
        lambda b, i: (b, GRID_W * jnp.clip(i * block_rows - HALO_ROWS, 0, rows - win_tokens // GRID_W), 0))

    def const_spec(shape):
        zeros = (0,) * len(shape)
        return pl.BlockSpec(shape, lambda b, i: zeros, pipeline_mode=pl.Buffered(1))

    stack_rows = HEADS_PER_STACK * GRID_W
    score_buf = pltpu.VMEM((N_STACKS, stack_rows, SCORE_W), jnp.float32)
    max_buf = pltpu.VMEM((N_STACKS, stack_rows, LANES), jnp.float32)
    prob_buf = pltpu.VMEM((N_STACKS, stack_rows, SCORE_W), jnp.bfloat16)
    kernel = functools.partial(_mixer_kernel, rows=rows, block_rows=block_rows)
    return pl.pallas_call(
        kernel,
        grid=(batch, rows // block_rows),
        in_specs=[
            main_spec(D_NA),
            win_spec,
            win_spec,
            prev_spec, main_spec(D_POOL), next_spec,
            main_spec(D_POOL),
            main_spec(D_NA),
            main_spec(D_MODEL),
            const_spec(u_meta.shape), const_spec(k_meta.shape), const_spec(v_meta.shape),
            const_spec(bias.shape), const_spec(bias_meta.shape),
            const_spec(w_pool.shape), const_spec(pool_scale.shape),
            const_spec(w_out.shape), const_spec(final_g.shape),
        ],
        out_specs=main_spec(D_MODEL),
        out_shape=jax.ShapeDtypeStruct(x.shape, x.dtype),
        scratch_shapes=[
            pltpu.VMEM((blk_tokens + 2 * POOL_PAD, D_POOL), jnp.bfloat16),
            pltpu.VMEM((blk_tokens, D_POOL), jnp.bfloat16),
            pltpu.VMEM((blk_tokens, D_MIX), jnp.bfloat16),
            score_buf, score_buf,
            max_buf, max_buf,
            prob_buf, prob_buf,
        ],
        compiler_params=pltpu.CompilerParams(
            dimension_semantics=("arbitrary", "arbitrary"), vmem_limit_bytes=VMEM_LIMIT_BYTES),
        name="mixer",
    )(q, k, v, u, u, u, gp, gn, x, u_meta, k_meta, v_meta, bias, bias_meta,
      w_pool, pool_scale, w_out, final_g)


def _bias_tables(rpb, meta_bias):
    qc = jnp.arange(GRID_W)[:, None]
    kc = jnp.arange(GRID_W)[None, :]
    cstart = jnp.clip(qc - NA_COLS // 2, 0, GRID_W - NA_COLS)
    valid = (kc >= cstart) & (kc < cstart + NA_COLS)
    onehot = ((kc - qc + (NA_COLS - 1))[:, :, None] == jnp.arange(2 * NA_COLS - 1)).astype(jnp.float32)
    rpb_st = rpb.reshape(N_STACKS, HEADS_PER_STACK, 2 * NA_ROWS - 1, 2 * NA_COLS - 1)
    dense = jnp.einsum("qkd,shod->sohqk", onehot, rpb_st, precision=lax.Precision.HIGHEST)
    dense = jnp.where(valid, dense, MASKED)
    pairs = jnp.concatenate([dense[:, :-1], dense[:, 1:]], axis=-1)
    bias = pairs.reshape(N_STACKS, PAIR_OFFSETS, HEADS_PER_STACK * GRID_W, 2 * GRID_W)
    mb = jnp.full((N_HEADS, META_PAD), MASKED, jnp.float32).at[:, :N_META].set(meta_bias)
    bias_meta = jnp.broadcast_to(mb[:, None, :], (N_HEADS, GRID_W, META_PAD)).reshape(
        N_STACKS, HEADS_PER_STACK * GRID_W, META_PAD)
    return bias.astype(jnp.float32), bias_meta


def _block_rows(rows):
    return 8


def kernel(x_prompt, x_sample, meta_tokens, norm_g, w_in, w_pool, pool_scale, rpb, meta_bias,
           w_out, final_g):
    assert norm_g.shape[0] == 1, "single-layer block"
    bf16 = jnp.bfloat16
    w_in_b = w_in[0].astype(bf16)
    w_pool_b = w_pool[0].astype(bf16)
    w_out_b = w_out[0].astype(bf16)
    g_in = norm_g[0][None, :]
    fg = final_g[None, :]
    pscale = pool_scale[0][None, :]
    bias, bias_meta = _bias_tables(rpb[0], meta_bias[0])

    u_m, _, _, k_m, v_m, _ = _in_proj(meta_tokens, g_in, w_in_b, N_META)
    pad = ((0, META_PAD - N_META), (0, 0))
    k_m = jnp.pad(k_m, pad)
    v_m = jnp.pad(v_m, pad)

    outs = []
    for x in (x_prompt, x_sample):
        batch, seq_tokens, _ = x.shape
        parts = _in_proj(x.reshape(batch * seq_tokens, D_MODEL), g_in, w_in_b, IN_PROJ_TOKENS)
        u, gp, q, k, v, gn = [p.reshape(batch, seq_tokens, -1) for p in parts]
        outs.append(_mixer(x, q, k, v, u, gp, gn, u_m, k_m, v_m, bias, bias_meta,
                           w_pool_b, pscale, w_out_b, fg, _block_rows(seq_tokens // GRID_W)))
    return tuple(outs)
```

```python
import functools

import jax
import jax.numpy as jnp
from jax import lax
from jax.experimental import pallas as pl
from jax.experimental.pallas import tpu as pltpu

D_MODEL = 1024
N_META = 16
GRID_W = 64
D_POOL = 1024
D_NA = 1024
D_MIX = D_POOL + D_NA
D_IN = 2 * D_POOL + 4 * D_NA
POOL_WINDOWS = (2, 4, 8, 16)
POOL_GROUP_W = D_POOL // len(POOL_WINDOWS)
HEAD_DIM = 64
N_HEADS = D_NA // HEAD_DIM
NA_ROWS = 8
NA_COLS = 16
RMS_EPS = 1e-6
MASKED = -1e30

LANES = 128
BF16_ROWS = 16
MXU_WIDTH = 256
VMEM_LIMIT_BYTES = 60 * 1024 * 1024

HEADS_PER_STACK = MXU_WIDTH // HEAD_DIM
STACK_W = HEADS_PER_STACK * HEAD_DIM
N_STACKS = N_HEADS // HEADS_PER_STACK
META_PAD = LANES
HALO_ROWS = NA_ROWS // 2
POOL_CHUNK = LANES
POOL_PAD = (MXU_WIDTH - POOL_CHUNK) // 2
PAIR_OFFSETS = 2 * NA_ROWS - 2
SCORE_W = NA_ROWS * GRID_W + META_PAD
IN_PROJ_TOKENS = 1024


def _silu(x):
    h = 0.5 * x
    return h + h * jnp.tanh(h)


def _in_proj_kernel(x_ref, g_ref, w_ref, u_ref, gp_ref, q_ref, k_ref, v_ref, gn_ref):
    x = x_ref[...]
    ms = jnp.mean(x * x, axis=-1, keepdims=True)
    h = (x * lax.rsqrt(ms + RMS_EPS) * g_ref[...]).astype(jnp.bfloat16)
    outs = (u_ref, gp_ref, q_ref, k_ref, v_ref, gn_ref)
    for n, o_ref in enumerate(outs):
        width = o_ref.shape[-1]
        acc = jnp.dot(h, w_ref[:, n * width:(n + 1) * width], preferred_element_type=jnp.float32)
        if o_ref is q_ref:
            acc = acc * (HEAD_DIM ** -0.5)
        o_ref[...] = acc.astype(o_ref.dtype)


def _in_proj(x2d, norm_g, w_in_bf16, block_tokens):
    tokens = x2d.shape[0]
    assert tokens % block_tokens == 0
    out = jax.ShapeDtypeStruct((tokens, D_POOL), jnp.bfloat16)
    row_spec = pl.BlockSpec((block_tokens, D_MODEL), lambda i: (i, 0))
    return pl.pallas_call(
        _in_proj_kernel,
        grid=(tokens // block_tokens,),
        in_specs=[
            row_spec,
            pl.BlockSpec((1, D_MODEL), lambda i: (0, 0)),
            pl.BlockSpec((D_MODEL, D_IN), lambda i: (0, 0), pipeline_mode=pl.Buffered(1)),
        ],
        out_specs=[row_spec] * 6,
        out_shape=[out] * 6,
        compiler_params=pltpu.CompilerParams(
            dimension_semantics=("arbitrary",), vmem_limit_bytes=VMEM_LIMIT_BYTES),
        name="in_proj",
    )(x2d, norm_g, w_in_bf16)


def _mixer_kernel(q_ref, kwin, vwin, up_ref, u_ref, un_ref,
                  gp_ref, gn_ref, x_ref, um_ref, km_ref, vm_ref, bias_ref, biasm_ref,
                  wpool_ref, pscale_ref, wout_ref, fg_ref, o_ref,
                  ubuf, pooled, mix, s_even, s_odd, m_even, m_odd, p_even, p_odd,
                  *, rows, block_rows):
    f32, bf16 = jnp.float32, jnp.bfloat16
    blk_tokens = block_rows * GRID_W
    halo_tokens = HALO_ROWS * GRID_W
    seq_tokens = rows * GRID_W
    i = pl.program_id(1)
    is_first = i == 0
    is_last = i == pl.num_programs(1) - 1

    win_row0 = jnp.clip(i * block_rows - HALO_ROWS, 0, rows - (block_rows + 2 * HALO_ROWS))

    ubuf[POOL_PAD:POOL_PAD + blk_tokens] = u_ref[...]

    @pl.when(is_first)
    def _():
        ubuf[0:POOL_PAD - N_META] = jnp.zeros((POOL_PAD - N_META, D_POOL), bf16)
        ubuf[POOL_PAD - N_META:POOL_PAD] = um_ref[...]

    @pl.when(jnp.logical_not(is_first))
    def _():
        ubuf[0:POOL_PAD] = up_ref[halo_tokens - POOL_PAD:halo_tokens, :]

    @pl.when(is_last)
    def _():
        ubuf[POOL_PAD + blk_tokens:] = jnp.zeros((POOL_PAD, D_POOL), bf16)

    @pl.when(jnp.logical_not(is_last))
    def _():
        ubuf[POOL_PAD + blk_tokens:] = un_ref[0:POOL_PAD, :]

    ctx_w = POOL_CHUNK + 2 * POOL_PAD
    out_tok = lax.broadcasted_iota(jnp.int32, (POOL_CHUNK, ctx_w), 0)
    ctx_tok = lax.broadcasted_iota(jnp.int32, (POOL_CHUNK, ctx_w), 1) - POOL_PAD
    bands = [((ctx_tok >= out_tok - w // 2) & (ctx_tok < out_tok + w // 2)).astype(bf16)
             for w in POOL_WINDOWS]
    for c in range(blk_tokens // POOL_CHUNK):
        t0 = c * POOL_CHUNK
        tok = i * blk_tokens + t0 + lax.broadcasted_iota(jnp.int32, (POOL_CHUNK, 1), 0)
        for g, w in enumerate(POOL_WINDOWS):
            lanes = slice(g * POOL_GROUP_W, (g + 1) * POOL_GROUP_W)
            win_sum = jnp.dot(bands[g], ubuf[t0:t0 + ctx_w, lanes], preferred_element_type=f32)
            inv_count = 1.0 / jnp.minimum(w, seq_tokens - tok + w // 2).astype(f32)
            centre = ubuf[POOL_PAD + t0:POOL_PAD + t0 + POOL_CHUNK, lanes].astype(f32)
            pooled[t0:t0 + POOL_CHUNK, lanes] = (win_sum * inv_count - centre).astype(bf16)

    for g in range(len(POOL_WINDOWS)):
        lanes = slice(g * POOL_GROUP_W, (g + 1) * POOL_GROUP_W)
        mixed = jnp.dot(pooled[:, lanes], wpool_ref[g], preferred_element_type=f32)
        mix[:, lanes] = (mixed * pscale_ref[:, lanes] * _silu(gp_ref[:, lanes].astype(f32))).astype(bf16)

    lane_head = lax.broadcasted_iota(jnp.int32, (1, STACK_W), 1) // HEAD_DIM
    nt_dims = (((1,), (1,)), ((), ()))
    n_keys = NA_ROWS * GRID_W

    def window_start(a):
        r = i * block_rows + a
        rs = jnp.clip(r - NA_ROWS // 2, 0, rows - NA_ROWS)
        key0 = pl.multiple_of((rs - win_row0) * GRID_W, GRID_W)
        return key0, rs - r + (NA_ROWS - 1)

    def score_stage(a, s_buf, m_buf, stacks=range(N_STACKS)):
        key0, first_off = window_start(a)
        tok0 = pl.multiple_of(a * GRID_W, GRID_W)
        for st in stacks:
            lanes = slice(st * STACK_W, (st + 1) * STACK_W)
            q4 = q_ref[pl.ds(tok0, GRID_W), lanes]
            qm = jnp.concatenate(
                [jnp.where(lane_head == h, q4, jnp.zeros_like(q4)) for h in range(HEADS_PER_STACK)],
                axis=0)
            s = lax.dot_general(qm, kwin[0, pl.ds(key0, n_keys), lanes], nt_dims,
                                preferred_element_type=f32)
            sm = lax.dot_general(qm, km_ref[:, lanes], nt_dims, preferred_element_type=f32)
            tile_max = sm + biasm_ref[st]
            s_buf[st, :, n_keys:] = tile_max
            for t in range(n_keys // LANES):
                cols = slice(t * LANES, (t + 1) * LANES)
                tile = s[:, cols] + bias_ref[st, first_off + 2 * t]
                s_buf[st, :, cols] = tile
                tile_max = jnp.maximum(tile_max, tile)
            m_buf[st] = jnp.broadcast_to(jnp.max(tile_max, axis=1, keepdims=True), tile_max.shape)

    def value_stage(a, s_buf, m_buf, p_buf, stacks=range(N_STACKS)):
        key0, _ = window_start(a)
        tok0 = pl.multiple_of(a * GRID_W, GRID_W)
        for st in stacks:
            lanes = slice(st * STACK_W, (st + 1) * STACK_W)
            partial = []
            for blk in range(HEADS_PER_STACK * GRID_W // BF16_ROWS):
                slab = slice(blk * BF16_ROWS, (blk + 1) * BF16_ROWS)
                m = m_buf[st, slab, :]
                acc = None
                for t in range(SCORE_W // LANES):
                    cols = slice(t * LANES, (t + 1) * LANES)
                    e = jnp.exp(s_buf[st, slab, cols] - m)
                    p_buf[st, slab, cols] = e.astype(bf16)
                    acc = e if acc is None else acc + e
                partial.append(acc)
            denom = jnp.sum(jnp.concatenate(partial, axis=0), axis=1, keepdims=True)
            o = jnp.dot(p_buf[st, :, :n_keys], vwin[0, pl.ds(key0, n_keys), lanes],
                        preferred_element_type=f32)
            o = o + jnp.dot(p_buf[st, :, n_keys:], vm_ref[:, lanes], preferred_element_type=f32)
            o = o * (1.0 / denom)
            na = o[0:GRID_W]
            for h in range(1, HEADS_PER_STACK):
                na = jnp.where(lane_head == h, o[h * GRID_W:(h + 1) * GRID_W], na)
            mix[pl.ds(tok0, GRID_W), D_POOL + st * STACK_W:D_POOL + (st + 1) * STACK_W] = (
                na * _silu(gn_ref[pl.ds(tok0, GRID_W), lanes].astype(f32))).astype(bf16)

    score_stage(0, s_even, m_even)

    def row_pair(j, carry):
        a = 2 * j
        for st in range(N_STACKS):
            score_stage(a + 1, s_odd, m_odd, (st,))
            value_stage(a, s_even, m_even, p_even, (st,))
        nxt = jnp.minimum(a + 2, block_rows - 1)
        for st in range(N_STACKS):
            score_stage(nxt, s_even, m_even, (st,))
            value_stage(a + 1, s_odd, m_odd, p_odd, (st,))
        return carry

    lax.fori_loop(0, block_rows // 2, row_pair, 0)

    y = x_ref[...] + jnp.dot(mix[...], wout_ref[...], preferred_element_type=f32)
    ms = jnp.mean(y * y, axis=-1, keepdims=True)
    o_ref[...] = y * lax.rsqrt(ms + RMS_EPS) * fg_ref[...]


def _mixer(x, q, k, v, u, gp, gn, u_meta, k_meta, v_meta, bias, bias_meta,
           w_pool, pool_scale, w_out, final_g, block_rows):
    batch, seq_tokens, _ = x.shape
    rows = seq_tokens // GRID_W
    assert rows * GRID_W == seq_tokens and rows >= NA_ROWS
    assert rows % block_rows == 0 and block_rows % HALO_ROWS == 0 and block_rows >= NA_ROWS
    assert block_rows % 2 == 0 and (block_rows * GRID_W) % POOL_CHUNK == 0
    blk_tokens = block_rows * GRID_W
    halo_tokens = HALO_ROWS * GRID_W
    halo_per_blk = block_rows // HALO_ROWS
    n_halo_blocks = rows // HALO_ROWS

    def main_spec(width):
        return pl.BlockSpec((None, blk_tokens, width), lambda b, i: (b, i, 0))

    prev_spec = pl.BlockSpec((None, halo_tokens, D_NA),
                             lambda b, i: (b, jnp.maximum(i * halo_per_blk - 1, 0), 0))
    next_spec = pl.BlockSpec((None, halo_tokens, D_NA),
                             lambda b, i: (b, jnp.minimum((i + 1) * halo_per_blk, n_halo_blocks - 1), 0))

    win_tokens = blk_tokens + 2 * halo_tokens
    assert seq_tokens >= win_tokens
    win_spec = pl.BlockSpec(
        (pl.Element(1), pl.Element(win_tokens), pl.Element(D_NA)),
        lambda b, i: (b, GRID_W * jnp.clip(i * block_rows - HALO_ROWS, 0, rows - win_tokens // GRID_W), 0))

    def const_spec(shape):
        zeros = (0,) * len(shape)
        return pl.BlockSpec(shape, lambda b, i: zeros, pipeline_mode=pl.Buffered(1))

    stack_rows = HEADS_PER_STACK * GRID_W
    score_buf = pltpu.VMEM((N_STACKS, stack_rows, SCORE_W), jnp.float32)
    max_buf = pltpu.VMEM((N_STACKS, stack_rows, LANES), jnp.float32)
    prob_buf = pltpu.VMEM((N_STACKS, stack_rows, SCORE_W), jnp.bfloat16)
    kernel = functools.partial(_mixer_kernel, rows=rows, block_rows=block_rows)
    return pl.pallas_call(
        kernel,
        grid=(batch, rows // block_rows),
        in_specs=[
            main_spec(D_NA),
            win_spec,
            win_spec,
            prev_spec, main_spec(D_POOL), next_spec,
            main_spec(D_POOL),
            main_spec(D_NA),
            main_spec(D_MODEL),
            const_spec(u_meta.shape), const_spec(k_meta.shape), const_spec(v_meta.shape),
            const_spec(bias.shape), const_spec(bias_meta.shape),
            const_spec(w_pool.shape), const_spec(pool_scale.shape),
            const_spec(w_out.shape), const_spec(final_g.shape),
        ],
        out_specs=main_spec(D_MODEL),
        out_shape=jax.ShapeDtypeStruct(x.shape, x.dtype),
        scratch_shapes=[
            pltpu.VMEM((blk_tokens + 2 * POOL_PAD, D_POOL), jnp.bfloat16),
            pltpu.VMEM((blk_tokens, D_POOL), jnp.bfloat16),
            pltpu.VMEM((blk_tokens, D_MIX), jnp.bfloat16),
            score_buf, score_buf,
            max_buf, max_buf,
            prob_buf, prob_buf,
        ],
        compiler_params=pltpu.CompilerParams(
            dimension_semantics=("arbitrary", "arbitrary"), vmem_limit_bytes=VMEM_LIMIT_BYTES),
        name="mixer",
    )(q, k, v, u, u, u, gp, gn, x, u_meta, k_meta, v_meta, bias, bias_meta,
      w_pool, pool_scale, w_out, final_g)


def _bias_tables(rpb, meta_bias):
    qc = jnp.arange(GRID_W)[:, None]
    kc = jnp.arange(GRID_W)[None, :]
    cstart = jnp.clip(qc - NA_COLS // 2, 0, GRID_W - NA_COLS)
    valid = (kc >= cstart) & (kc < cstart + NA_COLS)
    onehot = ((kc - qc + (NA_COLS - 1))[:, :, None] == jnp.arange(2 * NA_COLS - 1)).astype(jnp.float32)
    rpb_st = rpb.reshape(N_STACKS, HEADS_PER_STACK, 2 * NA_ROWS - 1, 2 * NA_COLS - 1)
    dense = jnp.einsum("qkd,shod->sohqk", onehot, rpb_st, precision=lax.Precision.HIGHEST)
    dense = jnp.where(valid, dense, MASKED)
    pairs = jnp.concatenate([dense[:, :-1], dense[:, 1:]], axis=-1)
    bias = pairs.reshape(N_STACKS, PAIR_OFFSETS, HEADS_PER_STACK * GRID_W, 2 * GRID_W)
    mb = jnp.full((N_HEADS, META_PAD), MASKED, jnp.float32).at[:, :N_META].set(meta_bias)
    bias_meta = jnp.broadcast_to(mb[:, None, :], (N_HEADS, GRID_W, META_PAD)).reshape(
        N_STACKS, HEADS_PER_STACK * GRID_W, META_PAD)
    return bias.astype(jnp.float32), bias_meta


def _block_rows(rows):
    return 8


def kernel(x_prompt, x_sample, meta_tokens, norm_g, w_in, w_pool, pool_scale, rpb, meta_bias,
           w_out, final_g):
    assert norm_g.shape[0] == 1, "single-layer block"
    bf16 = jnp.bfloat16
    w_in_b = w_in[0].astype(bf16)
    w_pool_b = w_pool[0].astype(bf16)
    w_out_b = w_out[0].astype(bf16)
    g_in = norm_g[0][None, :]
    fg = final_g[None, :]
    pscale = pool_scale[0][None, :]
    bias, bias_meta = _bias_tables(rpb[0], meta_bias[0])

    u_m, _, _, k_m, v_m, _ = _in_proj(meta_tokens, g_in, w_in_b, N_META)
    pad = ((0, META_PAD - N_META), (0, 0))
    k_m = jnp.pad(k_m, pad)
    v_m = jnp.pad(v_m, pad)

    outs = []
    for x in (x_prompt, x_sample):
        batch, seq_tokens, _ = x.shape
        parts = _in_proj(x.reshape(batch * seq_tokens, D_MODEL), g_in, w_in_b, IN_PROJ_TOKENS)
        u, gp, q, k, v, gn = [p.reshape(batch, seq_tokens, -1) for p in parts]
        outs.append(_mixer(x, q, k, v, u, gp, gn, u_m, k_m, v_m, bias, bias_meta,
                           w_pool_b, pscale, w_out_b, fg, _block_rows(seq_tokens // GRID_W)))
    return tuple(outs)
```

```python
import functools

import jax
import jax.numpy as jnp
from jax import lax
from jax.experimental import pallas as pl
from jax.experimental.pallas import tpu as pltpu

D_MODEL = 1024
N_META = 16
GRID_W = 64
D_POOL = 1024
D_NA = 1024
D_MIX = D_POOL + D_NA
D_IN = 2 * D_POOL + 4 * D_NA
POOL_WINDOWS = (2, 4, 8, 16)
POOL_GROUP_W = D_POOL // len(POOL_WINDOWS)
HEAD_DIM = 64
N_HEADS = D_NA // HEAD_DIM
NA_ROWS = 8
NA_COLS = 16
RMS_EPS = 1e-6
MASKED = -1e30

LANES = 128
BF16_ROWS = 16
MXU_WIDTH = 256
VMEM_LIMIT_BYTES = 60 * 1024 * 1024

HEADS_PER_STACK = MXU_WIDTH // HEAD_DIM
STACK_W = HEADS_PER_STACK * HEAD_DIM
N_STACKS = N_HEADS // HEADS_PER_STACK
META_PAD = LANES
HALO_ROWS = NA_ROWS // 2
POOL_CHUNK = LANES
POOL_PAD = (MXU_WIDTH - POOL_CHUNK) // 2
PAIR_OFFSETS = 2 * NA_ROWS - 2
SCORE_W = NA_ROWS * GRID_W + META_PAD
IN_PROJ_TOKENS = 1024


def _silu(x):
    h = 0.5 * x
    return h + h * jnp.tanh(h)


def _in_proj_kernel(x_ref, g_ref, w_ref, u_ref, gp_ref, q_ref, k_ref, v_ref, gn_ref):
    x = x_ref[...]
    ms = jnp.mean(x * x, axis=-1, keepdims=True)
    h = (x * lax.rsqrt(ms + RMS_EPS) * g_ref[...]).astype(jnp.bfloat16)
    outs = (u_ref, gp_ref, q_ref, k_ref, v_ref, gn_ref)
    for n, o_ref in enumerate(outs):
        width = o_ref.shape[-1]
        acc = jnp.dot(h, w_ref[:, n * width:(n + 1) * width], preferred_element_type=jnp.float32)
        if o_ref is q_ref:
            acc = acc * (HEAD_DIM ** -0.5)
        o_ref[...] = acc.astype(o_ref.dtype)


def _in_proj(x2d, norm_g, w_in_bf16, block_tokens):
    tokens = x2d.shape[0]
    assert tokens % block_tokens == 0
    out = jax.ShapeDtypeStruct((tokens, D_POOL), jnp.bfloat16)
    row_spec = pl.BlockSpec((block_tokens, D_MODEL), lambda i: (i, 0))
    return pl.pallas_call(
        _in_proj_kernel,
        grid=(tokens // block_tokens,),
        in_specs=[
            row_spec,
            pl.BlockSpec((1, D_MODEL), lambda i: (0, 0)),
            pl.BlockSpec((D_MODEL, D_IN), lambda i: (0, 0), pipeline_mode=pl.Buffered(1)),
        ],
        out_specs=[row_spec] * 6,
        out_shape=[out] * 6,
        compiler_params=pltpu.CompilerParams(
            dimension_semantics=("arbitrary",), vmem_limit_bytes=VMEM_LIMIT_BYTES),
        name="in_proj",
    )(x2d, norm_g, w_in_bf16)


def _mixer_kernel(q_ref, kwin, vwin, up_ref, u_ref, un_ref,
                  gp_ref, gn_ref, x_ref, um_ref, km_ref, vm_ref, bias_ref, biasm_ref,
                  wpool_ref, pscale_ref, wout_ref, fg_ref, o_ref,
                  ubuf, pooled, mix, s_even, s_odd, m_even, m_odd, p_even, p_odd,
                  *, rows, block_rows):
    f32, bf16 = jnp.float32, jnp.bfloat16
    blk_tokens = block_rows * GRID_W
    halo_tokens = HALO_ROWS * GRID_W
    seq_tokens = rows * GRID_W
    i = pl.program_id(1)
    is_first = i == 0
    is_last = i == pl.num_programs(1) - 1

    win_row0 = jnp.clip(i * block_rows - HALO_ROWS, 0, rows - (block_rows + 2 * HALO_ROWS))

    ubuf[POOL_PAD:POOL_PAD + blk_tokens] = u_ref[...]

    @pl.when(is_first)
    def _():
        ubuf[0:POOL_PAD - N_META] = jnp.zeros((POOL_PAD - N_META, D_POOL), bf16)
        ubuf[POOL_PAD - N_META:POOL_PAD] = um_ref[...]

    @pl.when(jnp.logical_not(is_first))
    def _():
        ubuf[0:POOL_PAD] = up_ref[halo_tokens - POOL_PAD:halo_tokens, :]

    @pl.when(is_last)
    def _():
        ubuf[POOL_PAD + blk_tokens:] = jnp.zeros((POOL_PAD, D_POOL), bf16)

    @pl.when(jnp.logical_not(is_last))
    def _():
        ubuf[POOL_PAD + blk_tokens:] = un_ref[0:POOL_PAD, :]

    ctx_w = POOL_CHUNK + 2 * POOL_PAD
    out_tok = lax.broadcasted_iota(jnp.int32, (POOL_CHUNK, ctx_w), 0)
    ctx_tok = lax.broadcasted_iota(jnp.int32, (POOL_CHUNK, ctx_w), 1) - POOL_PAD
    bands = [((ctx_tok >= out_tok - w // 2) & (ctx_tok < out_tok + w // 2)).astype(bf16)
             for w in POOL_WINDOWS]
    for c in range(blk_tokens // POOL_CHUNK):
        t0 = c * POOL_CHUNK
        tok = i * blk_tokens + t0 + lax.broadcasted_iota(jnp.int32, (POOL_CHUNK, 1), 0)
        for g, w in enumerate(POOL_WINDOWS):
            lanes = slice(g * POOL_GROUP_W, (g + 1) * POOL_GROUP_W)
            win_sum = jnp.dot(bands[g], ubuf[t0:t0 + ctx_w, lanes], preferred_element_type=f32)
            inv_count = 1.0 / jnp.minimum(w, seq_tokens - tok + w // 2).astype(f32)
            centre = ubuf[POOL_PAD + t0:POOL_PAD + t0 + POOL_CHUNK, lanes].astype(f32)
            pooled[t0:t0 + POOL_CHUNK, lanes] = (win_sum * inv_count - centre).astype(bf16)

    for g in range(len(POOL_WINDOWS)):
        lanes = slice(g * POOL_GROUP_W, (g + 1) * POOL_GROUP_W)
        mixed = jnp.dot(pooled[:, lanes], wpool_ref[g], preferred_element_type=f32)
        mix[:, lanes] = (mixed * pscale_ref[:, lanes] * _silu(gp_ref[:, lanes].astype(f32))).astype(bf16)

    lane_head = lax.broadcasted_iota(jnp.int32, (1, STACK_W), 1) // HEAD_DIM
    nt_dims = (((1,), (1,)), ((), ()))
    n_keys = NA_ROWS * GRID_W

    def window_start(a):
        r = i * block_rows + a
        rs = jnp.clip(r - NA_ROWS // 2, 0, rows - NA_ROWS)
        key0 = pl.multiple_of((rs - win_row0) * GRID_W, GRID_W)
        return key0, rs - r + (NA_ROWS - 1)

    def score_stage(a, s_buf, m_buf, stacks=range(N_STACKS)):
        key0, first_off = window_start(a)
        tok0 = a * GRID_W if isinstance(a, int) else pl.multiple_of(a * GRID_W, GRID_W)
        for st in stacks:
            lanes = slice(st * STACK_W, (st + 1) * STACK_W)
            q4 = q_ref[pl.ds(tok0, GRID_W), lanes]
            qm = jnp.concatenate(
                [jnp.where(lane_head == h, q4, jnp.zeros_like(q4)) for h in range(HEADS_PER_STACK)],
                axis=0)
            s = lax.dot_general(qm, kwin[0, pl.ds(key0, n_keys), lanes], nt_dims,
                                preferred_element_type=f32)
            sm = lax.dot_general(qm, km_ref[:, lanes], nt_dims, preferred_element_type=f32)
            tile_max = sm + biasm_ref[st]
            s_buf[st, :, n_keys:] = tile_max
            for t in range(n_keys // LANES):
                cols = slice(t * LANES, (t + 1) * LANES)
                tile = s[:, cols] + bias_ref[st, first_off + 2 * t]
                s_buf[st, :, cols] = tile
                tile_max = jnp.maximum(tile_max, tile)
            m_buf[st] = jnp.broadcast_to(jnp.max(tile_max, axis=1, keepdims=True), tile_max.shape)

    def value_stage(a, s_buf, m_buf, p_buf, stacks=range(N_STACKS)):
        key0, _ = window_start(a)
        tok0 = a * GRID_W if isinstance(a, int) else pl.multiple_of(a * GRID_W, GRID_W)
        for st in stacks:
            lanes = slice(st * STACK_W, (st + 1) * STACK_W)
            partial = []
            for blk in range(HEADS_PER_STACK * GRID_W // BF16_ROWS):
                slab = slice(blk * BF16_ROWS, (blk + 1) * BF16_ROWS)
                m = m_buf[st, slab, :]
                acc = None
                for t in range(SCORE_W // LANES):
                    cols = slice(t * LANES, (t + 1) * LANES)
                    e = jnp.exp(s_buf[st, slab, cols] - m)
                    p_buf[st, slab, cols] = e.astype(bf16)
                    acc = e if acc is None else acc + e
                partial.append(acc)
            denom = jnp.sum(jnp.concatenate(partial, axis=0), axis=1, keepdims=True)
            o = jnp.dot(p_buf[st, :, :n_keys], vwin[0, pl.ds(key0, n_keys), lanes],
                        preferred_element_type=f32)
            o = o + jnp.dot(p_buf[st, :, n_keys:], vm_ref[:, lanes], preferred_element_type=f32)
            o = o * (1.0 / denom)
            na = o[0:GRID_W]
            for h in range(1, HEADS_PER_STACK):
                na = jnp.where(lane_head == h, o[h * GRID_W:(h + 1) * GRID_W], na)
            mix[pl.ds(tok0, GRID_W), D_POOL + st * STACK_W:D_POOL + (st + 1) * STACK_W] = (
                na * _silu(gn_ref[pl.ds(tok0, GRID_W), lanes].astype(f32))).astype(bf16)

    score_stage(0, s_even, m_even)

    def row_pair(j, carry):
        a = 2 * j
        for st in range(N_STACKS):
            score_stage(a + 1, s_odd, m_odd, (st,))
            value_stage(a, s_even, m_even, p_even, (st,))
        for st in range(N_STACKS):
            score_stage(a + 2, s_even, m_even, (st,))
            value_stage(a + 1, s_odd, m_odd, p_odd, (st,))
        return carry

    lax.fori_loop(0, block_rows // 2 - 1, row_pair, 0)
    for st in range(N_STACKS):
        score_stage(block_rows - 1, s_odd, m_odd, (st,))
        value_stage(block_rows - 2, s_even, m_even, p_even, (st,))
    value_stage(block_rows - 1, s_odd, m_odd, p_odd)

    y = x_ref[...] + jnp.dot(mix[...], wout_ref[...], preferred_element_type=f32)
    ms = jnp.mean(y * y, axis=-1, keepdims=True)
    o_ref[...] = y * lax.rsqrt(ms + RMS_EPS) * fg_ref[...]


def _mixer(x, q, k, v, u, gp, gn, u_meta, k_meta, v_meta, bias, bias_meta,
           w_pool, pool_scale, w_out, final_g, block_rows):
    batch, seq_tokens, _ = x.shape
    rows = seq_tokens // GRID_W
    assert rows * GRID_W == seq_tokens and rows >= NA_ROWS
    assert rows % block_rows == 0 and block_rows % HALO_ROWS == 0 and block_rows >= NA_ROWS
    assert block_rows % 2 == 0 and (block_rows * GRID_W) % POOL_CHUNK == 0
    blk_tokens = block_rows * GRID_W
    halo_tokens = HALO_ROWS * GRID_W
    halo_per_blk = block_rows // HALO_ROWS
    n_halo_blocks = rows // HALO_ROWS

    def main_spec(width):
        return pl.BlockSpec((None, blk_tokens, width), lambda b, i: (b, i, 0))

    prev_spec = pl.BlockSpec((None, halo_tokens, D_NA),
                             lambda b, i: (b, jnp.maximum(i * halo_per_blk - 1, 0), 0))
    next_spec = pl.BlockSpec((None, halo_tokens, D_NA),
                             lambda b, i: (b, jnp.minimum((i + 1) * halo_per_blk, n_halo_blocks - 1), 0))

    win_tokens = blk_tokens + 2 * halo_tokens
    assert seq_tokens >= win_tokens
    win_spec = pl.BlockSpec(
        (pl.Element(1), pl.Element(win_tokens), pl.Element(D_NA)),
        lambda b, i: (b, GRID_W * jnp.clip(i * block_rows - HALO_ROWS, 0, rows - win_tokens // GRID_W), 0))

    def const_spec(shape):
        zeros = (0,) * len(shape)
        return pl.BlockSpec(shape, lambda b, i: zeros, pipeline_mode=pl.Buffered(1))

    stack_rows = HEADS_PER_STACK * GRID_W
    score_buf = pltpu.VMEM((N_STACKS, stack_rows, SCORE_W), jnp.float32)
    max_buf = pltpu.VMEM((N_STACKS, stack_rows, LANES), jnp.float32)
    prob_buf = pltpu.VMEM((N_STACKS, stack_rows, SCORE_W), jnp.bfloat16)
    kernel = functools.partial(_mixer_kernel, rows=rows, block_rows=block_rows)
    return pl.pallas_call(
        kernel,
        grid=(batch, rows // block_rows),
        in_specs=[
            main_spec(D_NA),
            win_spec,
            win_spec,
            prev_spec, main_spec(D_POOL), next_spec,
            main_spec(D_POOL),
            main_spec(D_NA),
            main_spec(D_MODEL),
            const_spec(u_meta.shape), const_spec(k_meta.shape), const_spec(v_meta.shape),
            const_spec(bias.shape), const_spec(bias_meta.shape),
            const_spec(w_pool.shape), const_spec(pool_scale.shape),
            const_spec(w_out.shape), const_spec(final_g.shape),
        ],
        out_specs=main_spec(D_MODEL),
        out_shape=jax.ShapeDtypeStruct(x.shape, x.dtype),
        scratch_shapes=[
            pltpu.VMEM((blk_tokens + 2 * POOL_PAD, D_POOL), jnp.bfloat16),
            pltpu.VMEM((blk_tokens, D_POOL), jnp.bfloat16),
            pltpu.VMEM((blk_tokens, D_MIX), jnp.bfloat16),
            score_buf, score_buf,
            max_buf, max_buf,
            prob_buf, prob_buf,
        ],
        compiler_params=pltpu.CompilerParams(
            dimension_semantics=("arbitrary", "arbitrary"), vmem_limit_bytes=VMEM_LIMIT_BYTES),
        name="mixer",
    )(q, k, v, u, u, u, gp, gn, x, u_meta, k_meta, v_meta, bias, bias_meta,
      w_pool, pool_scale, w_out, final_g)


def _bias_tables(rpb, meta_bias):
    qc = jnp.arange(GRID_W)[:, None]
    kc = jnp.arange(GRID_W)[None, :]
    cstart = jnp.clip(qc - NA_COLS // 2, 0, GRID_W - NA_COLS)
    valid = (kc >= cstart) & (kc < cstart + NA_COLS)
    onehot = ((kc - qc + (NA_COLS - 1))[:, :, None] == jnp.arange(2 * NA_COLS - 1)).astype(jnp.float32)
    rpb_st = rpb.reshape(N_STACKS, HEADS_PER_STACK, 2 * NA_ROWS - 1, 2 * NA_COLS - 1)
    dense = jnp.einsum("qkd,shod->sohqk", onehot, rpb_st, precision=lax.Precision.HIGHEST)
    dense = jnp.where(valid, dense, MASKED)
    pairs = jnp.concatenate([dense[:, :-1], dense[:, 1:]], axis=-1)
    bias = pairs.reshape(N_STACKS, PAIR_OFFSETS, HEADS_PER_STACK * GRID_W, 2 * GRID_W)
    mb = jnp.full((N_HEADS, META_PAD), MASKED, jnp.float32).at[:, :N_META].set(meta_bias)
    bias_meta = jnp.broadcast_to(mb[:, None, :], (N_HEADS, GRID_W, META_PAD)).reshape(
        N_STACKS, HEADS_PER_STACK * GRID_W, META_PAD)
    return bias.astype(jnp.float32), bias_meta


def _block_rows(rows):
    return 8


def kernel(x_prompt, x_sample, meta_tokens, norm_g, w_in, w_pool, pool_scale, rpb, meta_bias,
           w_out, final_g):
    assert norm_g.shape[0] == 1, "single-layer block"
    bf16 = jnp.bfloat16
    w_in_b = w_in[0].astype(bf16)
    w_pool_b = w_pool[0].astype(bf16)
    w_out_b = w_out[0].astype(bf16)
    g_in = norm_g[0][None, :]
    fg = final_g[None, :]
    pscale = pool_scale[0][None, :]
    bias, bias_meta = _bias_tables(rpb[0], meta_bias[0])

    u_m, _, _, k_m, v_m, _ = _in_proj(meta_tokens, g_in, w_in_b, N_META)
    pad = ((0, META_PAD - N_META), (0, 0))
    k_m = jnp.pad(k_m, pad)
    v_m = jnp.pad(v_m, pad)

    outs = []
    for x in (x_prompt, x_sample):
        batch, seq_tokens, _ = x.shape
        parts = _in_proj(x.reshape(batch * seq_tokens, D_MODEL), g_in, w_in_b, IN_PROJ_TOKENS)
        u, gp, q, k, v, gn = [p.reshape(batch, seq_tokens, -1) for p in parts]
        outs.append(_mixer(x, q, k, v, u, gp, gn, u_m, k_m, v_m, bias, bias_meta,
                           w_pool_b, pscale, w_out_b, fg, _block_rows(seq_tokens // GRID_W)))
    return tuple(outs)
```
